```python
import jax, jax.numpy as jnp
from jax import lax
import numpy as np

D_MODEL = 2048
BATCH = 16
SEQ = 2048
DEPTH = 4

CTX_LEN = 256
GRID_W = 64
HEAD_DIM = 128
D_FOURIER = D_MODEL // 4
FOURIER_GROUP_DIM = 128
N_FOURIER_GROUPS = D_FOURIER // FOURIER_GROUP_DIM
N_Q_HEADS = (D_MODEL - D_FOURIER) // HEAD_DIM
N_KV_HEADS = N_Q_HEADS // 3
GQA_GROUP = N_Q_HEADS // N_KV_HEADS
D_Q = N_Q_HEADS * HEAD_DIM
D_KV = N_KV_HEADS * HEAD_DIM
D_IN_PROJ = D_FOURIER + D_Q + 2 * D_KV
D_MIX = D_FOURIER + D_Q
WINDOW = 128
BLOCK = 128
KEY_SPAN = BLOCK + 2 * WINDOW
D_FF = ((8 * D_MODEL // 3 + 255) // 256) * 256
CONV_WIDTH = 3
ROPE_BASE = 10000.0
ROPE_AXIS_DIM = HEAD_DIM // 2
N_MOD = 6
LN_EPS = 1e-5
DEEPNORM_ALPHA = (2 * DEPTH) ** 0.25
DEEPNORM_BETA = (8 * DEPTH) ** -0.25
NEG_INF = -1e30
ATTN_SCALE = HEAD_DIM ** -0.5

kernel_name = "hybrid_fourier_window_gqa_convffn_deepnorm"


def _layernorm(x, g, b):
    xf = x.astype(jnp.float32)
    mu = jnp.mean(xf, axis=-1, keepdims=True)
    var = jnp.mean(jnp.square(xf - mu), axis=-1, keepdims=True)
    y = (xf - mu) * lax.rsqrt(var + LN_EPS) * g.astype(jnp.float32) + b.astype(jnp.float32)
    return y.astype(x.dtype)


def _axial_rope_angles(n_tokens):
    rows = n_tokens // GRID_W
    row = jnp.repeat(jnp.arange(rows, dtype=jnp.float32), GRID_W)
    col = jnp.tile(jnp.arange(GRID_W, dtype=jnp.float32), rows)
    inv = ROPE_BASE ** (-jnp.arange(0, ROPE_AXIS_DIM, 2, dtype=jnp.float32) / ROPE_AXIS_DIM)
    return row[:, None] * inv, col[:, None] * inv


def _rotate(x, ang):
    cos = jnp.cos(ang)[:, None, :].astype(x.dtype)
    sin = jnp.sin(ang)[:, None, :].astype(x.dtype)
    x1, x2 = jnp.split(x, 2, axis=-1)
    return jnp.concatenate([x1 * cos - x2 * sin, x2 * cos + x1 * sin], axis=-1)


def _apply_axial_rope(x, ang_row, ang_col):
    xr, xc = jnp.split(x, 2, axis=-1)
    return jnp.concatenate([_rotate(xr, ang_row), _rotate(xc, ang_col)], axis=-1)


def _fourier_mix(f, w_four):
    bsz, t = f.shape[0], f.shape[1]
    z = f.astype(jnp.float32).reshape(bsz, t, N_FOURIER_GROUPS, FOURIER_GROUP_DIM)
    y = jnp.fft.fft2(z, axes=(1, 3), norm="ortho").real
    return y.reshape(bsz, t, D_FOURIER).astype(f.dtype) @ w_four


def _sink_column(sink, lead_shape):
    s = sink.astype(jnp.float32).reshape(N_KV_HEADS, GQA_GROUP)
    return jnp.broadcast_to(s[None, :, :, None, None], lead_shape + (1,))


def _latent_window_attention(q, k, v, kc, vc, sink):
    bsz, t = q.shape[0], q.shape[1]
    nb = t // BLOCK
    kp = jnp.pad(k, ((0, 0), (WINDOW, WINDOW), (0, 0), (0, 0)))
    vp = jnp.pad(v, ((0, 0), (WINDOW, WINDOW), (0, 0), (0, 0)))
    qb = jnp.moveaxis(q.reshape(bsz, nb, BLOCK, N_KV_HEADS, GQA_GROUP, HEAD_DIM), 1, 0)
    qi = jnp.arange(BLOCK)[:, None]
    kj = jnp.arange(KEY_SPAN)[None, :]
    band = jnp.abs(kj - WINDOW - qi) <= WINDOW

    def one_block(args):
        qblk, b = args
        kblk = lax.dynamic_slice_in_dim(kp, b * BLOCK, KEY_SPAN, axis=1)
        vblk = lax.dynamic_slice_in_dim(vp, b * BLOCK, KEY_SPAN, axis=1)
        kpos = b * BLOCK + kj - WINDOW
        valid = band & (kpos >= 0) & (kpos < t)
        s_loc = jnp.einsum('bqhgd,bkhd->bhgqk', qblk, kblk).astype(jnp.float32) * ATTN_SCALE
        s_loc = jnp.where(valid, s_loc, NEG_INF)
        s_ctx = jnp.einsum('bqhgd,bchd->bhgqc', qblk, kc).astype(jnp.float32) * ATTN_SCALE
        s = jnp.concatenate([s_loc, s_ctx, _sink_column(sink, s_loc.shape[:-1])], axis=-1)
        p = jax.nn.softmax(s, axis=-1)
        p_loc = p[..., :KEY_SPAN].astype(v.dtype)
        p_ctx = p[..., KEY_SPAN:KEY_SPAN + kc.shape[1]].astype(v.dtype)
        return (jnp.einsum('bhgqk,bkhd->bqhgd', p_loc, vblk)
                + jnp.einsum('bhgqc,bchd->bqhgd', p_ctx, vc))

    out = lax.map(one_block, (qb, jnp.arange(nb)))
    return jnp.moveaxis(out, 0, 1).reshape(bsz, t, D_Q)


def _context_attention(qc, kc, vc, sink):
    bsz, n = qc.shape[0], qc.shape[1]
    s = jnp.einsum('bqhgd,bkhd->bhgqk', qc, kc).astype(jnp.float32) * ATTN_SCALE
    s = jnp.concatenate([s, _sink_column(sink, s.shape[:-1])], axis=-1)
    p = jax.nn.softmax(s, axis=-1)[..., :n].astype(vc.dtype)
    return jnp.einsum('bhgqk,bkhd->bqhgd', p, vc).reshape(bsz, n, D_Q)


def _split_proj(h, w_in):
    bsz, t = h.shape[0], h.shape[1]
    p = h @ w_in
    f, q, k, v = jnp.split(p, [D_FOURIER, D_FOURIER + D_Q, D_FOURIER + D_Q + D_KV], axis=-1)
    return (f, q.reshape(bsz, t, N_Q_HEADS, HEAD_DIM),
            k.reshape(bsz, t, N_KV_HEADS, HEAD_DIM), v.reshape(bsz, t, N_KV_HEADS, HEAD_DIM))


def _group_q(q):
    return q.reshape(q.shape[0], q.shape[1], N_KV_HEADS, GQA_GROUP, HEAD_DIM)


def _conv_ffn(h, w_up, conv_w, conv_b, w_down):
    u = h @ w_up
    u = lax.conv_general_dilated(
        u, conv_w[:, None, :].astype(u.dtype), window_strides=(1,), padding='SAME',
        dimension_numbers=('NWC', 'WIO', 'NWC'), feature_group_count=2 * D_FF) + conv_b
    a, g = jnp.split(u, 2, axis=-1)
    return (jax.nn.silu(g) * a) @ w_down


def setup_inputs(seed: int = 0) -> dict:
    key = jax.random.key(seed)
    ks = jax.random.split(key, 16)
    f32 = jnp.float32
    nrm = lambda k, shape, s: jax.random.normal(k, shape, f32) * s
    return {
        "x": nrm(ks[0], (BATCH, SEQ, D_MODEL), 1.0),
        "c": nrm(ks[1], (BATCH, D_MODEL), 1.0),
        "ctx": nrm(ks[2], (BATCH, CTX_LEN, D_MODEL), 1.0),
        "c_ctx": nrm(ks[3], (D_MODEL,), 1.0),
        "w_ada": nrm(ks[4], (DEPTH, D_MODEL, N_MOD * D_MODEL), 0.5 * D_MODEL ** -0.5),
        "b_ada": nrm(ks[5], (DEPTH, N_MOD * D_MODEL), 0.02),
        "w_in": nrm(ks[6], (DEPTH, D_MODEL, D_IN_PROJ), D_MODEL ** -0.5),
        "sink": nrm(ks[7], (DEPTH, N_Q_HEADS), 0.5),
        "w_four": nrm(ks[8], (DEPTH, D_FOURIER, D_FOURIER), D_FOURIER ** -0.5),
        "w_out": nrm(ks[9], (DEPTH, D_MIX, D_MODEL), DEEPNORM_BETA * D_MIX ** -0.5),
        "ln_g": 1.0 + nrm(ks[10], (DEPTH, 2, D_MODEL), 0.02),
        "ln_b": nrm(ks[11], (DEPTH, 2, D_MODEL), 0.02),
        "w_up": nrm(ks[12], (DEPTH, D_MODEL, 2 * D_FF), D_MODEL ** -0.5),
        "conv_w": nrm(ks[13], (DEPTH, CONV_WIDTH, 2 * D_FF), CONV_WIDTH ** -0.5),
        "conv_b": nrm(ks[14], (DEPTH, 2 * D_FF), 0.02),
        "w_down": nrm(ks[15], (DEPTH, D_FF, D_MODEL), DEEPNORM_BETA * D_FF ** -0.5),
    }


def reference(x, c, ctx, c_ctx, w_ada, b_ada, w_in, sink, w_four, w_out, ln_g, ln_b,
              w_up, conv_w, conv_b, w_down):
    ang_row, ang_col = _axial_rope_angles(x.shape[1])
    silu_c = jax.nn.silu(c)
    silu_cc = jax.nn.silu(c_ctx)
    for l in range(DEPTH):
        last = l == DEPTH - 1
        m = (silu_c @ w_ada[l] + b_ada[l])[:, None, :]
        sh_a, sc_a, g_a, sh_f, sc_f, g_f = jnp.split(m, N_MOD, axis=-1)
        mc = silu_cc @ w_ada[l] + b_ada[l]
        shc_a, scc_a, gc_a, shc_f, scc_f, gc_f = jnp.split(mc, N_MOD, axis=-1)

        h = x * (1.0 + sc_a) + sh_a
        hc = ctx * (1.0 + scc_a) + shc_a
        f, q, k, v = _split_proj(h, w_in[l])
        q = _group_q(_apply_axial_rope(q, ang_row, ang_col))
        k = _apply_axial_rope(k, ang_row, ang_col)
        if last:
            kvc = hc @ w_in[l][:, D_FOURIER + D_Q:]
            kc, vc = jnp.split(kvc, 2, axis=-1)
            kc = kc.reshape(hc.shape[0], hc.shape[1], N_KV_HEADS, HEAD_DIM)
            vc = vc.reshape(hc.shape[0], hc.shape[1], N_KV_HEADS, HEAD_DIM)
        else:
            fc, qc, kc, vc = _split_proj(hc, w_in[l])
        y = jnp.concatenate([_fourier_mix(f, w_four[l]),
                             _latent_window_attention(q, k, v, kc, vc, sink[l])], axis=-1) @ w_out[l]
        x = _layernorm(DEEPNORM_ALPHA * x + g_a * y, ln_g[l, 0], ln_b[l, 0])
        if not last:
            yc = jnp.concatenate([_fourier_mix(fc, w_four[l]),
                                  _context_attention(_group_q(qc), kc, vc, sink[l])], axis=-1) @ w_out[l]
            ctx = _layernorm(DEEPNORM_ALPHA * ctx + gc_a * yc, ln_g[l, 0], ln_b[l, 0])

        h = x * (1.0 + sc_f) + sh_f
        x = _layernorm(DEEPNORM_ALPHA * x + g_f * _conv_ffn(h, w_up[l], conv_w[l], conv_b[l], w_down[l]),
                       ln_g[l, 1], ln_b[l, 1])
        if not last:
            hc = ctx * (1.0 + scc_f) + shc_f
            ctx = _layernorm(DEEPNORM_ALPHA * ctx + gc_f * _conv_ffn(hc, w_up[l], conv_w[l], conv_b[l], w_down[l]),
                             ln_g[l, 1], ln_b[l, 1])
    return x
```

```python
import functools

import numpy as np
import jax
import jax.numpy as jnp
from jax import lax
from jax.experimental import pallas as pl
from jax.experimental.pallas import tpu as pltpu

F32 = jnp.float32
BF16 = jnp.bfloat16

HEAD_DIM = 128
FOURIER_GROUP_DIM = 128
GQA_GROUP = 3
WINDOW = 128
GRID_W = 64
ROPE_BASE = 10000.0
N_MOD = 6
LN_EPS = 1e-5
NEG_INF = -1e30
SUBLANES = 8

VMEM_LIMIT_BYTES = 56 * 1024 * 1024


def _cparams(*sem):
    return pltpu.CompilerParams(dimension_semantics=sem, vmem_limit_bytes=VMEM_LIMIT_BYTES)


def _dot(a, b):
    return jnp.dot(a, b, preferred_element_type=F32)


def _dot_nt(a, b):
    return lax.dot_general(a, b, (((1,), (1,)), ((), ())), preferred_element_type=F32)


def _silu(v):
    return v * jax.nn.sigmoid(v)


def _ada_kernel(c_ref, w_ref, b_ref, o_ref):
    s = _silu(c_ref[...]).astype(BF16)
    o_ref[...] = _dot(s, w_ref[...].astype(BF16)) + b_ref[...]


def _ada_mods(c_all, w_ada, b_ada, tn):
    n_layers, d, nd = w_ada.shape
    r = c_all.shape[0]
    return pl.pallas_call(
        _ada_kernel,
        grid=(n_layers, nd // tn),
        in_specs=[
            pl.BlockSpec((r, d), lambda l, j: (0, 0)),
            pl.BlockSpec((None, d, tn), lambda l, j: (l, 0, j)),
            pl.BlockSpec((None, 1, tn), lambda l, j: (l, 0, j)),
        ],
        out_specs=pl.BlockSpec((None, r, tn), lambda l, j: (l, 0, j)),
        out_shape=jax.ShapeDtypeStruct((n_layers, r, nd), F32),
        compiler_params=_cparams("parallel", "parallel"),
        name="ada_mods",
    )(c_all, w_ada, b_ada.reshape(n_layers, 1, nd))


def _modulate_kernel(x_ref, sc_ref, sh_ref, o_ref):
    o_ref[...] = (x_ref[...] * (1.0 + sc_ref[...]) + sh_ref[...]).astype(BF16)


def _inproj_kernel(n_rope_slabs, cw, h_ref, w_ref, cos_ref, sa_ref, sb_ref, o_ref):
    h = h_ref[...]
    n_cols = w_ref.shape[1]
    slabs_per_chunk = cw // HEAD_DIM
    for jc in range(n_cols // cw):
        acc = _dot(h, w_ref[:, jc * cw:(jc + 1) * cw])
        for s in range(slabs_per_chunk):
            v = acc[:, s * HEAD_DIM:(s + 1) * HEAD_DIM]
            if jc * slabs_per_chunk + s < n_rope_slabs:
                v = (v * cos_ref[...]
                     + pltpu.roll(v, HEAD_DIM // 4, 1) * sa_ref[...]
                     + pltpu.roll(v, 3 * HEAD_DIM // 4, 1) * sb_ref[...])
            c0 = jc * cw + s * HEAD_DIM
            o_ref[:, c0:c0 + HEAD_DIM] = v.astype(BF16)


def _fourier_kernel(t, n_groups, rt, scale, f_ref, c_ref, s_ref, dft_ref, wf_ref, o_ref, w1_ref):
    for g in range(n_groups):
        cols = slice(g * FOURIER_GROUP_DIM, (g + 1) * FOURIER_GROUP_DIM)
        z = f_ref[:, cols]
        w1_ref[0:t, cols] = _dot(z, c_ref[...]).astype(BF16)
        w1_ref[t:2 * t, cols] = _dot(z, s_ref[...]).astype(BF16)
    for r in range(t // rt):
        rows = slice(r * rt, (r + 1) * rt)
        y = _dot(dft_ref[rows, :], w1_ref[...]) * scale
        o_ref[rows, :] = _dot(y.astype(BF16), wf_ref[...]).astype(BF16)


def _attn_kernel(t, tq, n_ctx, local, scale, sink_ref, q_ref, *refs):
    if local:
        k_ref, v_ref, kc_ref, vc_ref, o_ref = refs
    else:
        kc_ref, vc_ref, o_ref = refs
    hkv = pl.program_id(1)
    span = 3 * tq
    kc = kc_ref[...]
    vc = vc_ref[...]
    sink_col = jnp.concatenate(
        [jnp.full((tq, 1), sink_ref[hkv * GQA_GROUP + g], F32) for g in range(GQA_GROUP)], axis=0)

    def one_block(j, carry):
        q0 = pl.multiple_of(j * tq, tq)
        qb = q_ref[pl.ds(q0, tq), :]
        q3 = jnp.concatenate(
            [qb[:, g * HEAD_DIM:(g + 1) * HEAD_DIM] for g in range(GQA_GROUP)], axis=0)
        s_ctx = _dot_nt(q3, kc) * scale
        m = jnp.maximum(jnp.max(s_ctx, axis=-1, keepdims=True), sink_col)
        if local:
            start = pl.multiple_of(jnp.clip((j - 1) * tq, 0, t - span), tq)
            ks = k_ref[pl.ds(start, span), :]
            vs = v_ref[pl.ds(start, span), :]
            s_loc = _dot_nt(q3, ks) * scale
            qi = lax.broadcasted_iota(jnp.int32, (tq, span), 0)
            kj = lax.broadcasted_iota(jnp.int32, (tq, span), 1)
            valid1 = jnp.abs(kj - qi + (start - q0)) <= WINDOW
            valid = jnp.concatenate([valid1] * GQA_GROUP, axis=0)
            s_loc = jnp.where(valid, s_loc, NEG_INF)
            m = jnp.maximum(m, jnp.max(s_loc, axis=-1, keepdims=True))
        e_ctx = jnp.exp(s_ctx - m)
        den = jnp.sum(e_ctx, axis=-1, keepdims=True) + jnp.exp(sink_col - m)
        o = _dot(e_ctx.astype(BF16), vc)
        if local:
            e_loc = jnp.exp(s_loc - m)
            den = den + jnp.sum(e_loc, axis=-1, keepdims=True)
            o = o + _dot(e_loc.astype(BF16), vs)
        o = o / den
        ob = jnp.concatenate([o[g * tq:(g + 1) * tq] for g in range(GQA_GROUP)], axis=1)
        o_ref[pl.ds(q0, tq), :] = ob.astype(BF16)
        return carry

    lax.fori_loop(0, t // tq, one_block, 0)


def _ln_epilogue(alpha, y, x_ref, gate_ref, lng_ref, lnb_ref, mod_refs, out_refs):
    z = alpha * x_ref[...] + gate_ref[...] * y
    mu = jnp.mean(z, axis=-1, keepdims=True)
    zc = z - mu
    var = jnp.mean(zc * zc, axis=-1, keepdims=True)
    xn = zc * lax.rsqrt(var + LN_EPS) * lng_ref[...] + lnb_ref[...]
    out_refs[0][...] = xn
    if mod_refs:
        sc_ref, sh_ref = mod_refs
        out_refs[1][...] = (xn * (1.0 + sc_ref[...]) + sh_ref[...]).astype(BF16)


def _outproj_ln_kernel(alpha, with_h, ya_ref, yf_ref, wa_ref, wf_ref, x_ref, gate_ref, lng_ref,
                       lnb_ref, *rest):
    mod_refs, out_refs = (rest[:2], rest[2:]) if with_h else ((), rest)
    y = _dot(ya_ref[...], wa_ref[...]) + _dot(yf_ref[...], wf_ref[...])
    _ln_epilogue(alpha, y, x_ref, gate_ref, lng_ref, lnb_ref, mod_refs, out_refs)


def _down_ln_kernel(alpha, with_h, a_ref, w_ref, x_ref, gate_ref, lng_ref, lnb_ref, *rest):
    acc_ref = rest[-1]
    rest = rest[:-1]
    mod_refs, out_refs = (rest[:2], rest[2:]) if with_h else ((), rest)
    k = pl.program_id(1)
    part = _dot(a_ref[...], w_ref[...])

    @pl.when(k == 0)
    def _():
        acc_ref[...] = part

    @pl.when(k > 0)
    def _():
        acc_ref[...] += part

    @pl.when(k == pl.num_programs(1) - 1)
    def _():
        _ln_epilogue(alpha, acc_ref[...], x_ref, gate_ref, lng_ref, lnb_ref, mod_refs, out_refs)


def _upconv_kernel(period, x_ref, wa_ref, wg_ref, cwa_ref, cwg_ref, cba_ref, cbg_ref, o_ref):
    x = x_ref[...]
    rows = x.shape[0]
    ua = _dot(x, wa_ref[...])
    ug = _dot(x, wg_ref[...])

    def conv(prev, cur, nxt, cw_ref, cb_ref):
        return prev * cw_ref[0:1, :] + cur * cw_ref[1:2, :] + nxt * cw_ref[2:3, :] + cb_ref[...]

    def gated(pa, ca, na, pg, cg, ng):
        return (_silu(conv(pg, cg, ng, cwg_ref, cbg_ref)) * conv(pa, ca, na, cwa_ref, cba_ref))

    pa, na = pltpu.roll(ua, 1, 0), pltpu.roll(ua, rows - 1, 0)
    pg, ng = pltpu.roll(ug, 1, 0), pltpu.roll(ug, rows - 1, 0)
    o_ref[...] = gated(pa, ua, na, pg, ug, ng).astype(BF16)

    sub = lax.broadcasted_iota(jnp.int32, (SUBLANES, ua.shape[1]), 0)
    for s in range(rows // period):
        lo, hi = s * period, (s + 1) * period
        head, tail = slice(lo, lo + SUBLANES), slice(hi - SUBLANES, hi)
        zero_first = lambda v: jnp.where(sub == 0, 0.0, pltpu.roll(v, 1, 0))
        zero_last = lambda v: jnp.where(sub == SUBLANES - 1, 0.0, pltpu.roll(v, SUBLANES - 1, 0))
        o_ref[head, :] = gated(zero_first(ua[head]), ua[head], na[head],
                               zero_first(ug[head]), ug[head], ng[head]).astype(BF16)
        o_ref[tail, :] = gated(pa[tail], ua[tail], zero_last(ua[tail]),
                               pg[tail], ug[tail], zero_last(ug[tail])).astype(BF16)


def _rope_tables(t, n_ident):
    rows = t // GRID_W
    row = jnp.repeat(jnp.arange(rows, dtype=F32), GRID_W)
    col = jnp.tile(jnp.arange(GRID_W, dtype=F32), rows)
    axis_dim = HEAD_DIM // 2
    inv = ROPE_BASE ** (-jnp.arange(0, axis_dim, 2, dtype=F32) / axis_dim)
    ang = jnp.concatenate([row[:, None] * inv, row[:, None] * inv,
                           col[:, None] * inv, col[:, None] * inv], axis=-1)
    lane = np.arange(HEAD_DIM)
    second_half = jnp.asarray((lane // (axis_dim // 2)) % 2 == 1)
    cos, sin = jnp.cos(ang), jnp.sin(ang)
    sa = jnp.where(second_half, sin, 0.0)
    sb = jnp.where(second_half, 0.0, -sin)
    pad = lambda a, v: jnp.concatenate([a, jnp.full((n_ident, HEAD_DIM), v, F32)], axis=0)
    return pad(cos, 1.0), pad(sa, 0.0), pad(sb, 0.0)


def _dft_constants(t):
    def cs(n):
        k = np.arange(n, dtype=np.int64)
        ph = 2.0 * np.pi * ((k[:, None] * k[None, :]) % n).astype(np.float64) / n
        return np.cos(ph), np.sin(ph)
    ct, st = cs(t)
    cg, sg = cs(FOURIER_GROUP_DIM)
    to_bf16 = lambda a: jnp.asarray(a.astype(np.float32)).astype(BF16)
    return to_bf16(np.concatenate([ct, -st], axis=1)), to_bf16(cg), to_bf16(sg)


def kernel(x, c, ctx, c_ctx, w_ada, b_ada, w_in, sink, w_four, w_out, ln_g, ln_b, w_up, conv_w,
           conv_b, w_down):
    bsz, t, d = x.shape
    n_ctx = ctx.shape[1]
    n_layers = w_ada.shape[0]
    d_f = w_four.shape[1]
    n_groups = d_f // FOURIER_GROUP_DIM
    d_q = d - d_f
    n_q = d_q // HEAD_DIM
    n_kv = n_q // GQA_GROUP
    d_kv = n_kv * HEAD_DIM
    d_in = d_f + d_q + 2 * d_kv
    d_ff = w_down.shape[1]
    alpha = (2 * n_layers) ** 0.25
    scale = HEAD_DIM ** -0.5

    n_lat, n_cx = bsz * t, bsz * n_ctx
    n_tok = n_lat + n_cx
    tm = min(1024, t, n_cx)
    tq = WINDOW
    assert t % tm == 0 and n_cx % tm == 0
    assert t % n_ctx == 0 and n_cx % t == 0 and t >= 3 * tq and t % GRID_W == 0
    assert w_in.shape[2] == d_in and d_ff % 256 == 0
    tiles_per_seq = t // tm
    lat_tiles = n_lat // tm
    cx_blk0 = n_lat // n_ctx

    w_in_b = jnp.concatenate([w_in[:, :, d_f:], w_in[:, :, :d_f]], axis=-1).astype(BF16)
    w_out_f = w_out[:, :d_f].astype(BF16)
    w_out_a = w_out[:, d_f:].astype(BF16)
    w_four_b = w_four.astype(BF16)
    w_up_b = w_up.astype(BF16)
    w_down_b = w_down.astype(BF16)
    ln_g4 = ln_g.reshape(n_layers, 2, 1, d)
    ln_b4 = ln_b.reshape(n_layers, 2, 1, d)
    conv_b3 = conv_b.reshape(n_layers, 1, 2 * d_ff)

    cos_t, sa_t, sb_t = _rope_tables(t, tm)
    dft_lat = _dft_constants(t)
    dft_cx = _dft_constants(n_ctx)

    n_mod_rows = -(-(bsz + 1) // SUBLANES) * SUBLANES
    c_all = jnp.concatenate(
        [c, c_ctx[None, :], jnp.zeros((n_mod_rows - bsz - 1, d), F32)], axis=0)
    mods = _ada_mods(c_all, w_ada, b_ada, tn=min(1024, d)).reshape(n_layers, n_mod_rows, 1, N_MOD * d)

    def mod_row(i, tile, n_lat_tiles):
        return jnp.where(i < n_lat_tiles, i // (t // tile), bsz)

    def mod_spec(l, chunk, tile, n_lat_tiles, grid_rank=1):
        if grid_rank == 1:
            return pl.BlockSpec((None, None, 1, d),
                                lambda i: (l, mod_row(i, tile, n_lat_tiles), 0, chunk))
        return pl.BlockSpec((None, None, 1, d),
                            lambda i, k: (l, mod_row(i, tile, n_lat_tiles), 0, chunk))

    def ln_spec(l, which, grid_rank=1):
        if grid_rank == 1:
            return pl.BlockSpec((None, None, 1, d), lambda i: (l, which, 0, 0))
        return pl.BlockSpec((None, None, 1, d), lambda i, k: (l, which, 0, 0))

    xs = jnp.concatenate([x.reshape(n_lat, d), ctx.reshape(n_cx, d)], axis=0)

    h = pl.pallas_call(
        _modulate_kernel,
        grid=(n_tok // tm,),
        in_specs=[pl.BlockSpec((tm, d), lambda i: (i, 0)),
                  mod_spec(0, 1, tm, lat_tiles), mod_spec(0, 0, tm, lat_tiles)],
        out_specs=pl.BlockSpec((tm, d), lambda i: (i, 0)),
        out_shape=jax.ShapeDtypeStruct((n_tok, d), BF16),
        compiler_params=_cparams("parallel"),
        name="modulate0",
    )(xs, mods, mods)

    for l in range(n_layers):
        last = l == n_layers - 1
        rows_out = n_lat if last else n_tok

        qkvf = pl.pallas_call(
            functools.partial(_inproj_kernel, (d_q + d_kv) // HEAD_DIM,
                              next(w for w in (512, 256, 128) if d_in % w == 0)),
            grid=(n_tok // tm,),
            in_specs=[
                pl.BlockSpec((tm, d), lambda i: (i, 0)),
                pl.BlockSpec((None, d, d_in), lambda i: (l, 0, 0), pipeline_mode=pl.Buffered(1)),
            ] + [pl.BlockSpec((tm, HEAD_DIM),
                              lambda i: (jnp.where(i < lat_tiles, i % tiles_per_seq, tiles_per_seq), 0))
                 ] * 3,
            out_specs=pl.BlockSpec((tm, d_in), lambda i: (i, 0)),
            out_shape=jax.ShapeDtypeStruct((n_tok, d_in), BF16),
            compiler_params=_cparams("parallel"),
            name=f"inproj{l}",
        )(h, w_in_b, cos_t, sa_t, sb_t)

        f_col = (d_q + 2 * d_kv) // d_f

        def fourier(seq, n_seq, blk0, consts, prev):
            dft, cg, sg = consts
            rt = min(512, seq)
            args = [qkvf, cg, sg, dft, w_four_b]
            in_specs = [
                pl.BlockSpec((seq, d_f), lambda b: (blk0 + b, f_col)),
                pl.BlockSpec((FOURIER_GROUP_DIM, FOURIER_GROUP_DIM), lambda b: (0, 0)),
                pl.BlockSpec((FOURIER_GROUP_DIM, FOURIER_GROUP_DIM), lambda b: (0, 0)),
                pl.BlockSpec((seq, 2 * seq), lambda b: (0, 0), pipeline_mode=pl.Buffered(1)),
                pl.BlockSpec((None, d_f, d_f), lambda b: (l, 0, 0)),
            ]
            aliases = {}
            if prev is not None:
                args.append(prev)
                in_specs.append(pl.BlockSpec(memory_space=pl.ANY))
                aliases = {len(args) - 1: 0}
            kern = functools.partial(_fourier_kernel, seq, n_groups, rt,
                                     1.0 / float(np.sqrt(seq * FOURIER_GROUP_DIM)))
            if prev is not None:
                kern_inner = kern
                kern = lambda f, cgr, sgr, dr, wr, _prev, o, w1: kern_inner(f, cgr, sgr, dr, wr, o, w1)
            return pl.pallas_call(
                kern,
                grid=(n_seq,),
                in_specs=in_specs,
                out_specs=pl.BlockSpec((seq, d_f), lambda b: (blk0 + b, 0)),
                out_shape=jax.ShapeDtypeStruct((rows_out, d_f), BF16),
                scratch_shapes=[pltpu.VMEM((2 * seq, d_f), BF16)],
                input_output_aliases=aliases,
                compiler_params=_cparams("parallel"),
                name=f"fourier{l}_{seq}",
            )(*args)

        yf = fourier(t, bsz, 0, dft_lat, None)
        if not last:
            yf = fourier(n_ctx, bsz, cx_blk0, dft_cx, yf)

        kc_spec = pl.BlockSpec((n_ctx, HEAD_DIM), lambda b, hh: (cx_blk0 + b, n_q + hh))
        vc_spec = pl.BlockSpec((n_ctx, HEAD_DIM), lambda b, hh: (cx_blk0 + b, n_q + n_kv + hh))
        smem_spec = pl.BlockSpec(memory_space=pltpu.SMEM)
        ya = pl.pallas_call(
            functools.partial(_attn_kernel, t, tq, n_ctx, True, scale),
            grid=(bsz, n_kv),
            in_specs=[
                smem_spec,
                pl.BlockSpec((t, GQA_GROUP * HEAD_DIM), lambda b, hh: (b, hh)),
                pl.BlockSpec((t, HEAD_DIM), lambda b, hh: (b, n_q + hh)),
                pl.BlockSpec((t, HEAD_DIM), lambda b, hh: (b, n_q + n_kv + hh)),
                kc_spec, vc_spec,
            ],
            out_specs=pl.BlockSpec((t, GQA_GROUP * HEAD_DIM), lambda b, hh: (b, hh)),
            out_shape=jax.ShapeDtypeStruct((rows_out, d_q), BF16),
            compiler_params=_cparams("parallel", "parallel"),
            name=f"attn{l}",
        )(sink[l], qkvf, qkvf, qkvf, qkvf, qkvf)
        if not last:
            ctx_kern = functools.partial(_attn_kernel, n_ctx, min(tq, n_ctx), n_ctx, False, scale)
            ya = pl.pallas_call(
                lambda s_r, q_r, kc_r, vc_r, _prev, o_r: ctx_kern(s_r, q_r, kc_r, vc_r, o_r),
                grid=(bsz, n_kv),
                in_specs=[
                    smem_spec,
                    pl.BlockSpec((n_ctx, GQA_GROUP * HEAD_DIM), lambda b, hh: (cx_blk0 + b, hh)),
                    kc_spec, vc_spec,
                    pl.BlockSpec(memory_space=pl.ANY),
                ],
                out_specs=pl.BlockSpec((n_ctx, GQA_GROUP * HEAD_DIM), lambda b, hh: (cx_blk0 + b, hh)),
                out_shape=jax.ShapeDtypeStruct((rows_out, d_q), BF16),
                input_output_aliases={4: 0},
                compiler_params=_cparams("parallel", "parallel"),
                name=f"attn_ctx{l}",
            )(sink[l], qkvf, qkvf, qkvf, ya)

        tp = min(512, tm)
        lat_tiles_p = n_lat // tp
        xs, h = pl.pallas_call(
            functools.partial(_outproj_ln_kernel, alpha, True),
            grid=(rows_out // tp,),
            in_specs=[
                pl.BlockSpec((tp, d_q), lambda i: (i, 0)),
                pl.BlockSpec((tp, d_f), lambda i: (i, 0)),
                pl.BlockSpec((None, d_q, d), lambda i: (l, 0, 0), pipeline_mode=pl.Buffered(1)),
                pl.BlockSpec((None, d_f, d), lambda i: (l, 0, 0), pipeline_mode=pl.Buffered(1)),
                pl.BlockSpec((tp, d), lambda i: (i, 0)),
                mod_spec(l, 2, tp, lat_tiles_p), ln_spec(l, 0), ln_spec(l, 0),
                mod_spec(l, 4, tp, lat_tiles_p), mod_spec(l, 3, tp, lat_tiles_p),
            ],
            out_specs=[pl.BlockSpec((tp, d), lambda i: (i, 0)),
                       pl.BlockSpec((tp, d), lambda i: (i, 0))],
            out_shape=[jax.ShapeDtypeStruct((rows_out, d), F32),
                       jax.ShapeDtypeStruct((rows_out, d), BF16)],
            compiler_params=_cparams("parallel"),
            name=f"outproj_ln{l}",
        )(ya, yf, w_out_a, w_out_f, xs, mods, ln_g4, ln_b4, mods, mods)

        tn = 256
        n_colt = d_ff // tn

        def upconv(period, n_tiles, tile0, prev):
            args = [h, w_up_b, w_up_b, conv_w, conv_w, conv_b3, conv_b3]
            in_specs = [
                pl.BlockSpec((t, d), lambda i, j: (tile0 + i, 0)),
                pl.BlockSpec((None, d, tn), lambda i, j: (l, 0, j)),
                pl.BlockSpec((None, d, tn), lambda i, j: (l, 0, n_colt + j)),
                pl.BlockSpec((None, 3, tn), lambda i, j: (l, 0, j)),
                pl.BlockSpec((None, 3, tn), lambda i, j: (l, 0, n_colt + j)),
                pl.BlockSpec((None, 1, tn), lambda i, j: (l, 0, j)),
                pl.BlockSpec((None, 1, tn), lambda i, j: (l, 0, n_colt + j)),
            ]
            kern = functools.partial(_upconv_kernel, period)
            aliases = {}
            if prev is not None:
                args.append(prev)
                in_specs.append(pl.BlockSpec(memory_space=pl.ANY))
                aliases = {len(args) - 1: 0}
                kern_inner = kern
                kern = lambda *r: kern_inner(*r[:7], r[8])
            return pl.pallas_call(
                kern,
                grid=(n_tiles, n_colt),
                in_specs=in_specs,
                out_specs=pl.BlockSpec((t, tn), lambda i, j: (tile0 + i, j)),
                out_shape=jax.ShapeDtypeStruct((rows_out, d_ff), BF16),
                input_output_aliases=aliases,
                compiler_params=_cparams("parallel", "arbitrary"),
                name=f"upconv{l}_{period}",
            )(*args)

        act = upconv(t, bsz, 0, None)
        if not last:
            act = upconv(n_ctx, n_cx // t, bsz, act)

        tk = 512
        td = tp
        lat_tiles_d = n_lat // td
        in_specs = [
            pl.BlockSpec((td, tk), lambda i, k: (i, k)),
            pl.BlockSpec((None, tk, d), lambda i, k: (l, k, 0)),
            pl.BlockSpec((td, d), lambda i, k: (i, 0)),
            mod_spec(l, 5, td, lat_tiles_d, 2), ln_spec(l, 1, 2), ln_spec(l, 1, 2),
        ]
        args = [act, w_down_b, xs, mods, ln_g4, ln_b4]
        out_specs = [pl.BlockSpec((td, d), lambda i, k: (i, 0))]
        out_shape = [jax.ShapeDtypeStruct((rows_out, d), F32)]
        if not last:
            in_specs += [mod_spec(l + 1, 1, td, lat_tiles_d, 2), mod_spec(l + 1, 0, td, lat_tiles_d, 2)]
            args += [mods, mods]
            out_specs.append(pl.BlockSpec((td, d), lambda i, k: (i, 0)))
            out_shape.append(jax.ShapeDtypeStruct((rows_out, d), BF16))
        res = pl.pallas_call(
            functools.partial(_down_ln_kernel, alpha, not last),
            grid=(rows_out // td, d_ff // tk),
            in_specs=in_specs,
            out_specs=out_specs,
            out_shape=out_shape,
            scratch_shapes=[pltpu.VMEM((td, d), F32)],
            compiler_params=_cparams("parallel", "arbitrary"),
            name=f"down_ln{l}",
        )(*args)
        if last:
            xs = res[0]
        else:
            xs, h = res

    return xs.reshape(bsz, t, d)
```

```python
import functools

import numpy as np
import jax
import jax.numpy as jnp
from jax import lax
from jax.experimental import pallas as pl
from jax.experimental.pallas import tpu as pltpu

F32 = jnp.float32
BF16 = jnp.bfloat16

HEAD_DIM = 128
FOURIER_GROUP_DIM = 128
GQA_GROUP = 3
WINDOW = 128
GRID_W = 64
ROPE_BASE = 10000.0
N_MOD = 6
LN_EPS = 1e-5
NEG_INF = -1e30
SUBLANES = 8

VMEM_LIMIT_BYTES = 56 * 1024 * 1024


def _cparams(*sem):
    return pltpu.CompilerParams(dimension_semantics=sem, vmem_limit_bytes=VMEM_LIMIT_BYTES)


def _dot(a, b):
    return jnp.dot(a, b, preferred_element_type=F32)


def _dot_nt(a, b):
    return lax.dot_general(a, b, (((1,), (1,)), ((), ())), preferred_element_type=F32)


def _silu(v):
    return v * jax.nn.sigmoid(v)


def _ada_kernel(c_ref, w_ref, b_ref, o_ref):
    s = _silu(c_ref[...]).astype(BF16)
    o_ref[...] = _dot(s, w_ref[...].astype(BF16)) + b_ref[...]


def _ada_mods(c_all, w_ada, b_ada, tn):
    n_layers, d, nd = w_ada.shape
    r = c_all.shape[0]
    return pl.pallas_call(
        _ada_kernel,
        grid=(n_layers, nd // tn),
        in_specs=[
            pl.BlockSpec((r, d), lambda l, j: (0, 0)),
            pl.BlockSpec((None, d, tn), lambda l, j: (l, 0, j)),
            pl.BlockSpec((None, 1, tn), lambda l, j: (l, 0, j)),
        ],
        out_specs=pl.BlockSpec((None, r, tn), lambda l, j: (l, 0, j)),
        out_shape=jax.ShapeDtypeStruct((n_layers, r, nd), F32),
        compiler_params=_cparams("parallel", "parallel"),
        name="ada_mods",
    )(c_all, w_ada, b_ada.reshape(n_layers, 1, nd))


def _modulate_kernel(lat_tiles, xl_ref, xc_ref, sc_ref, sh_ref, o_ref):
    x = jnp.where(pl.program_id(0) < lat_tiles, xl_ref[...], xc_ref[...])
    o_ref[...] = (x * (1.0 + sc_ref[...]) + sh_ref[...]).astype(BF16)


def _inproj_kernel(n_rope_slabs, cw, h_ref, w_ref, cos_ref, sa_ref, sb_ref, o_ref):
    h = h_ref[...]
    n_cols = w_ref.shape[1]
    slabs_per_chunk = cw // HEAD_DIM
    for jc in range(n_cols // cw):
        acc = _dot(h, w_ref[:, jc * cw:(jc + 1) * cw])
        for s in range(slabs_per_chunk):
            v = acc[:, s * HEAD_DIM:(s + 1) * HEAD_DIM]
            if jc * slabs_per_chunk + s < n_rope_slabs:
                v = (v * cos_ref[...]
                     + pltpu.roll(v, HEAD_DIM // 4, 1) * sa_ref[...]
                     + pltpu.roll(v, 3 * HEAD_DIM // 4, 1) * sb_ref[...])
            c0 = jc * cw + s * HEAD_DIM
            o_ref[:, c0:c0 + HEAD_DIM] = v.astype(BF16)


def _fourier_kernel(t, n_groups, rt, scale, f_ref, c_ref, s_ref, dft_ref, wf_ref, o_ref, w1_ref):
    for g in range(n_groups):
        cols = slice(g * FOURIER_GROUP_DIM, (g + 1) * FOURIER_GROUP_DIM)
        z = f_ref[:, cols]
        w1_ref[0:t, cols] = _dot(z, c_ref[...]).astype(BF16)
        w1_ref[t:2 * t, cols] = _dot(z, s_ref[...]).astype(BF16)
    for r in range(t // rt):
        rows = slice(r * rt, (r + 1) * rt)
        y = _dot(dft_ref[rows, :], w1_ref[...]) * scale
        o_ref[rows, :] = _dot(y.astype(BF16), wf_ref[...]).astype(BF16)


def _attn_kernel(t, tq, n_ctx, local, scale, sink_ref, q_ref, *refs):
    if local:
        k_ref, v_ref, kc_ref, vc_ref, o_ref = refs
    else:
        kc_ref, vc_ref, o_ref = refs
    hkv = pl.program_id(1)
    span = 3 * tq
    kc = kc_ref[...]
    vc = vc_ref[...]
    sink_col = jnp.concatenate(
        [jnp.full((tq, 1), sink_ref[hkv * GQA_GROUP + g], F32) for g in range(GQA_GROUP)], axis=0)

    def one_block(j, carry):
        q0 = pl.multiple_of(j * tq, tq)
        qb = q_ref[pl.ds(q0, tq), :]
        q3 = jnp.concatenate(
            [qb[:, g * HEAD_DIM:(g + 1) * HEAD_DIM] for g in range(GQA_GROUP)], axis=0)
        s = _dot_nt(q3, kc) * scale
        v_all = vc
        if local:
            start = pl.multiple_of(jnp.clip((j - 1) * tq, 0, t - span), tq)
            ks = k_ref[pl.ds(start, span), :]
            s_loc = _dot_nt(q3, ks) * scale
            qi = lax.broadcasted_iota(jnp.int32, (tq, span), 0)
            kj = lax.broadcasted_iota(jnp.int32, (tq, span), 1)
            valid1 = jnp.abs(kj - qi + (start - q0)) <= WINDOW
            valid = jnp.concatenate([valid1] * GQA_GROUP, axis=0)
            s = jnp.concatenate([jnp.where(valid, s_loc, NEG_INF), s], axis=1)
            v_all = jnp.concatenate([v_ref[pl.ds(start, span), :], vc], axis=0)
        m = jnp.maximum(jnp.max(s, axis=-1, keepdims=True), sink_col)
        e = jnp.exp(s - m)
        den = jnp.sum(e, axis=-1, keepdims=True) + jnp.exp(sink_col - m)
        o = _dot(e.astype(BF16), v_all) / den
        ob = jnp.concatenate([o[g * tq:(g + 1) * tq] for g in range(GQA_GROUP)], axis=1)
        o_ref[pl.ds(q0, tq), :] = ob.astype(BF16)
        return carry

    n_blocks = t // tq
    lax.fori_loop(0, n_blocks, one_block, 0, unroll=2 if n_blocks % 2 == 0 else 1)


LN_ROW_CHUNK = 128


def _ln_rows(alpha, y, x, gate_ref, lng_ref, lnb_ref, mod_refs, out_refs, rows):
    z = alpha * x + gate_ref[...] * y
    mu = jnp.mean(z, axis=-1, keepdims=True)
    zc = z - mu
    var = jnp.mean(zc * zc, axis=-1, keepdims=True)
    xn = zc * lax.rsqrt(var + LN_EPS) * lng_ref[...] + lnb_ref[...]
    out_refs[0][rows, :] = xn
    if mod_refs:
        sc_ref, sh_ref = mod_refs
        out_refs[1][rows, :] = (xn * (1.0 + sc_ref[...]) + sh_ref[...]).astype(BF16)


def _outproj_ln_kernel(alpha, n_x, lat_tiles, ya_ref, yf_ref, wa_ref, wf_ref, *rest):
    x_refs, (gate_ref, lng_ref, lnb_ref, sc_ref, sh_ref, xo_ref, ho_ref) = rest[:n_x], rest[n_x:]
    if n_x == 2:
        x = jnp.where(pl.program_id(0) < lat_tiles, x_refs[0][...], x_refs[1][...])
    else:
        x = x_refs[0][...]
    y = _dot(ya_ref[...], wa_ref[...]) + _dot(yf_ref[...], wf_ref[...])
    _ln_rows(alpha, y, x, gate_ref, lng_ref, lnb_ref, (sc_ref, sh_ref), (xo_ref, ho_ref),
             slice(None))


def _down_ln_kernel(alpha, with_h, a_ref, w_ref, x_ref, gate_ref, lng_ref, lnb_ref, *rest):
    acc_ref = rest[-1]
    rest = rest[:-1]
    mod_refs, out_refs = (rest[:2], rest[2:]) if with_h else ((), rest)
    k = pl.program_id(1)

    @pl.when(k == 0)
    def _():
        acc_ref[...] = _dot(a_ref[...], w_ref[...])

    @pl.when(k > 0)
    def _():
        acc_ref[...] += _dot(a_ref[...], w_ref[...])

    @pl.when(k == pl.num_programs(1) - 1)
    def _():
        rc = min(LN_ROW_CHUNK, acc_ref.shape[0])

        def chunk(r, carry):
            rows = pl.ds(pl.multiple_of(r * rc, rc), rc)
            _ln_rows(alpha, acc_ref[rows, :], x_ref[rows, :], gate_ref, lng_ref, lnb_ref, mod_refs,
                     out_refs, rows)
            return carry

        lax.fori_loop(0, acc_ref.shape[0] // rc, chunk, 0)


UP_GROUP = 256


def _interleave_value_gate(a, d_ff):
    lead = a.shape[:-1]
    v = a[..., :d_ff].reshape(lead + (d_ff // UP_GROUP, 1, UP_GROUP))
    g = a[..., d_ff:].reshape(lead + (d_ff // UP_GROUP, 1, UP_GROUP))
    return jnp.concatenate([v, g], axis=-2).reshape(lead + (2 * d_ff,))


def _upconv_kernel(period, x_ref, w_ref, cw_ref, cb_ref, o_ref):
    x = x_ref[...]
    rows = x.shape[0]
    gw = 2 * UP_GROUP
    sub = lax.broadcasted_iota(jnp.int32, (SUBLANES, gw), 0)
    for c in range(w_ref.shape[1] // gw):
        cols = slice(c * gw, (c + 1) * gw)
        ocols = slice(c * UP_GROUP, (c + 1) * UP_GROUP)
        u = _dot(x, w_ref[:, cols])
        cw0, cw1, cw2, cb = cw_ref[0:1, cols], cw_ref[1:2, cols], cw_ref[2:3, cols], cb_ref[:, cols]

        def gated(prev, cur, nxt):
            v = prev * cw0 + cur * cw1 + nxt * cw2 + cb
            return (_silu(v[:, UP_GROUP:]) * v[:, :UP_GROUP]).astype(BF16)

        up, dn = pltpu.roll(u, 1, 0), pltpu.roll(u, rows - 1, 0)
        o_ref[:, ocols] = gated(up, u, dn)
        for s in range(rows // period):
            lo, hi = s * period, (s + 1) * period
            head, tail = slice(lo, lo + SUBLANES), slice(hi - SUBLANES, hi)
            up8 = jnp.where(sub == 0, 0.0, pltpu.roll(u[head], 1, 0))
            dn8 = jnp.where(sub == SUBLANES - 1, 0.0, pltpu.roll(u[tail], SUBLANES - 1, 0))
            o_ref[head, ocols] = gated(up8, u[head], dn[head])
            o_ref[tail, ocols] = gated(up[tail], u[tail], dn8)


def _rope_tables(t, n_ident):
    rows = t // GRID_W
    row = jnp.repeat(jnp.arange(rows, dtype=F32), GRID_W)
    col = jnp.tile(jnp.arange(GRID_W, dtype=F32), rows)
    axis_dim = HEAD_DIM // 2
    inv = ROPE_BASE ** (-jnp.arange(0, axis_dim, 2, dtype=F32) / axis_dim)
    ang = jnp.concatenate([row[:, None] * inv, row[:, None] * inv,
                           col[:, None] * inv, col[:, None] * inv], axis=-1)
    lane = np.arange(HEAD_DIM)
    second_half = jnp.asarray((lane // (axis_dim // 2)) % 2 == 1)
    cos, sin = jnp.cos(ang), jnp.sin(ang)
    sa = jnp.where(second_half, sin, 0.0)
    sb = jnp.where(second_half, 0.0, -sin)
    pad = lambda a, v: jnp.concatenate([a, jnp.full((n_ident, HEAD_DIM), v, F32)], axis=0)
    return pad(cos, 1.0), pad(sa, 0.0), pad(sb, 0.0)


def _dft_constants(t):
    def cs(n):
        k = np.arange(n, dtype=np.int64)
        ph = 2.0 * np.pi * ((k[:, None] * k[None, :]) % n).astype(np.float64) / n
        return np.cos(ph), np.sin(ph)
    ct, st = cs(t)
    cg, sg = cs(FOURIER_GROUP_DIM)
    to_bf16 = lambda a: jnp.asarray(a.astype(np.float32)).astype(BF16)
    return to_bf16(np.concatenate([ct, -st], axis=1)), to_bf16(cg), to_bf16(sg)


def kernel(x, c, ctx, c_ctx, w_ada, b_ada, w_in, sink, w_four, w_out, ln_g, ln_b, w_up, conv_w,
           conv_b, w_down):
    bsz, t, d = x.shape
    n_ctx = ctx.shape[1]
    n_layers = w_ada.shape[0]
    d_f = w_four.shape[1]
    n_groups = d_f // FOURIER_GROUP_DIM
    d_q = d - d_f
    n_q = d_q // HEAD_DIM
    n_kv = n_q // GQA_GROUP
    d_kv = n_kv * HEAD_DIM
    d_in = d_f + d_q + 2 * d_kv
    d_ff = w_down.shape[1]
    alpha = (2 * n_layers) ** 0.25
    scale = HEAD_DIM ** -0.5

    n_lat, n_cx = bsz * t, bsz * n_ctx
    n_tok = n_lat + n_cx
    tm = min(1024, t, n_cx)
    tq = WINDOW
    assert t % tm == 0 and n_cx % tm == 0
    assert t % n_ctx == 0 and n_cx % t == 0 and t >= 3 * tq and t % GRID_W == 0
    assert w_in.shape[2] == d_in and d_ff % 256 == 0
    tiles_per_seq = t // tm
    lat_tiles = n_lat // tm
    cx_blk0 = n_lat // n_ctx

    w_in_b = jnp.concatenate([w_in[:, :, d_f:], w_in[:, :, :d_f]], axis=-1).astype(BF16)
    w_out_f = w_out[:, :d_f].astype(BF16)
    w_out_a = w_out[:, d_f:].astype(BF16)
    w_four_b = w_four.astype(BF16)
    w_up_b = _interleave_value_gate(w_up, d_ff).astype(BF16)
    w_down_b = w_down.astype(BF16)
    ln_g4 = ln_g.reshape(n_layers, 2, 1, d)
    ln_b4 = ln_b.reshape(n_layers, 2, 1, d)
    conv_w3 = _interleave_value_gate(conv_w, d_ff)
    conv_b3 = _interleave_value_gate(conv_b, d_ff).reshape(n_layers, 1, 2 * d_ff)

    cos_t, sa_t, sb_t = _rope_tables(t, tm)
    dft_lat = _dft_constants(t)
    dft_cx = _dft_constants(n_ctx)

    n_mod_rows = -(-(bsz + 1) // SUBLANES) * SUBLANES
    c_all = jnp.concatenate(
        [c, c_ctx[None, :], jnp.zeros((n_mod_rows - bsz - 1, d), F32)], axis=0)
    mods = _ada_mods(c_all, w_ada, b_ada, tn=min(1024, d)).reshape(n_layers, n_mod_rows, 1, N_MOD * d)

    def mod_row(i, tile, n_lat_tiles):
        return jnp.where(i < n_lat_tiles, i // (t // tile), bsz)

    def mod_spec(l, chunk, tile, n_lat_tiles, grid_rank=1):
        if grid_rank == 1:
            return pl.BlockSpec((None, None, 1, d),
                                lambda i: (l, mod_row(i, tile, n_lat_tiles), 0, chunk))
        return pl.BlockSpec((None, None, 1, d),
                            lambda i, k: (l, mod_row(i, tile, n_lat_tiles), 0, chunk))

    def ln_spec(l, which, grid_rank=1):
        if grid_rank == 1:
            return pl.BlockSpec((None, None, 1, d), lambda i: (l, which, 0, 0))
        return pl.BlockSpec((None, None, 1, d), lambda i, k: (l, which, 0, 0))

    x_lat, x_cx = x.reshape(n_lat, d), ctx.reshape(n_cx, d)

    def two_source_specs(tile):
        n_lat_tiles = n_lat // tile
        return [pl.BlockSpec((tile, d), lambda i: (jnp.minimum(i, n_lat_tiles - 1), 0)),
                pl.BlockSpec((tile, d), lambda i: (jnp.maximum(i - n_lat_tiles, 0), 0))]

    h = pl.pallas_call(
        functools.partial(_modulate_kernel, lat_tiles),
        grid=(n_tok // tm,),
        in_specs=two_source_specs(tm) + [mod_spec(0, 1, tm, lat_tiles), mod_spec(0, 0, tm, lat_tiles)],
        out_specs=pl.BlockSpec((tm, d), lambda i: (i, 0)),
        out_shape=jax.ShapeDtypeStruct((n_tok, d), BF16),
        compiler_params=_cparams("parallel"),
        name="modulate0",
    )(x_lat, x_cx, mods, mods)
    xs = None

    for l in range(n_layers):
        last = l == n_layers - 1
        rows_out = n_lat if last else n_tok

        qkvf = pl.pallas_call(
            functools.partial(_inproj_kernel, (d_q + d_kv) // HEAD_DIM,
                              next(w for w in (512, 256, 128) if d_in % w == 0)),
            grid=(n_tok // tm,),
            in_specs=[
                pl.BlockSpec((tm, d), lambda i: (i, 0)),
                pl.BlockSpec((None, d, d_in), lambda i: (l, 0, 0), pipeline_mode=pl.Buffered(1)),
            ] + [pl.BlockSpec((tm, HEAD_DIM),
                              lambda i: (jnp.where(i < lat_tiles, i % tiles_per_seq, tiles_per_seq), 0))
                 ] * 3,
            out_specs=pl.BlockSpec((tm, d_in), lambda i: (i, 0)),
            out_shape=jax.ShapeDtypeStruct((n_tok, d_in), BF16),
            compiler_params=_cparams("parallel"),
            name=f"inproj{l}",
        )(h, w_in_b, cos_t, sa_t, sb_t)

        f_col = (d_q + 2 * d_kv) // d_f

        def fourier(seq, n_seq, blk0, consts, prev):
            dft, cg, sg = consts
            rt = min(512, seq)
            args = [qkvf, cg, sg, dft, w_four_b]
            in_specs = [
                pl.BlockSpec((seq, d_f), lambda b: (blk0 + b, f_col)),
                pl.BlockSpec((FOURIER_GROUP_DIM, FOURIER_GROUP_DIM), lambda b: (0, 0)),
                pl.BlockSpec((FOURIER_GROUP_DIM, FOURIER_GROUP_DIM), lambda b: (0, 0)),
                pl.BlockSpec((seq, 2 * seq), lambda b: (0, 0), pipeline_mode=pl.Buffered(1)),
                pl.BlockSpec((None, d_f, d_f), lambda b: (l, 0, 0)),
            ]
            aliases = {}
            if prev is not None:
                args.append(prev)
                in_specs.append(pl.BlockSpec(memory_space=pl.ANY))
                aliases = {len(args) - 1: 0}
            kern = functools.partial(_fourier_kernel, seq, n_groups, rt,
                                     1.0 / float(np.sqrt(seq * FOURIER_GROUP_DIM)))
            if prev is not None:
                kern_inner = kern
                kern = lambda f, cgr, sgr, dr, wr, _prev, o, w1: kern_inner(f, cgr, sgr, dr, wr, o, w1)
            return pl.pallas_call(
                kern,
                grid=(n_seq,),
                in_specs=in_specs,
                out_specs=pl.BlockSpec((seq, d_f), lambda b: (blk0 + b, 0)),
                out_shape=jax.ShapeDtypeStruct((rows_out, d_f), BF16),
                scratch_shapes=[pltpu.VMEM((2 * seq, d_f), BF16)],
                input_output_aliases=aliases,
                compiler_params=_cparams("parallel"),
                name=f"fourier{l}_{seq}",
            )(*args)

        yf = fourier(t, bsz, 0, dft_lat, None)
        if not last:
            yf = fourier(n_ctx, bsz, cx_blk0, dft_cx, yf)

        kc_spec = pl.BlockSpec((n_ctx, HEAD_DIM), lambda b, hh: (cx_blk0 + b, n_q + hh))
        vc_spec = pl.BlockSpec((n_ctx, HEAD_DIM), lambda b, hh: (cx_blk0 + b, n_q + n_kv + hh))
        smem_spec = pl.BlockSpec(memory_space=pltpu.SMEM)
        ya = pl.pallas_call(
            functools.partial(_attn_kernel, t, tq, n_ctx, True, scale),
            grid=(bsz, n_kv),
            in_specs=[
                smem_spec,
                pl.BlockSpec((t, GQA_GROUP * HEAD_DIM), lambda b, hh: (b, hh)),
                pl.BlockSpec((t, HEAD_DIM), lambda b, hh: (b, n_q + hh)),
                pl.BlockSpec((t, HEAD_DIM), lambda b, hh: (b, n_q + n_kv + hh)),
                kc_spec, vc_spec,
            ],
            out_specs=pl.BlockSpec((t, GQA_GROUP * HEAD_DIM), lambda b, hh: (b, hh)),
            out_shape=jax.ShapeDtypeStruct((rows_out, d_q), BF16),
            compiler_params=_cparams("parallel", "parallel"),
            name=f"attn{l}",
        )(sink[l], qkvf, qkvf, qkvf, qkvf, qkvf)
        if not last:
            ctx_kern = functools.partial(_attn_kernel, n_ctx, min(tq, n_ctx), n_ctx, False, scale)
            ya = pl.pallas_call(
                lambda s_r, q_r, kc_r, vc_r, _prev, o_r: ctx_kern(s_r, q_r, kc_r, vc_r, o_r),
                grid=(bsz, n_kv),
                in_specs=[
                    smem_spec,
                    pl.BlockSpec((n_ctx, GQA_GROUP * HEAD_DIM), lambda b, hh: (cx_blk0 + b, hh)),
                    kc_spec, vc_spec,
                    pl.BlockSpec(memory_space=pl.ANY),
                ],
                out_specs=pl.BlockSpec((n_ctx, GQA_GROUP * HEAD_DIM), lambda b, hh: (cx_blk0 + b, hh)),
                out_shape=jax.ShapeDtypeStruct((rows_out, d_q), BF16),
                input_output_aliases={4: 0},
                compiler_params=_cparams("parallel", "parallel"),
                name=f"attn_ctx{l}",
            )(sink[l], qkvf, qkvf, qkvf, ya)

        tp = min(512, tm)
        lat_tiles_p = n_lat // tp
        if xs is None:
            x_args, x_specs = [x_lat, x_cx], two_source_specs(tp)
        else:
            x_args, x_specs = [xs], [pl.BlockSpec((tp, d), lambda i: (i, 0))]
        xs, h = pl.pallas_call(
            functools.partial(_outproj_ln_kernel, alpha, len(x_args), lat_tiles_p),
            grid=(rows_out // tp,),
            in_specs=[
                pl.BlockSpec((tp, d_q), lambda i: (i, 0)),
                pl.BlockSpec((tp, d_f), lambda i: (i, 0)),
                pl.BlockSpec((None, d_q, d), lambda i: (l, 0, 0), pipeline_mode=pl.Buffered(1)),
                pl.BlockSpec((None, d_f, d), lambda i: (l, 0, 0), pipeline_mode=pl.Buffered(1)),
            ] + x_specs + [
                mod_spec(l, 2, tp, lat_tiles_p), ln_spec(l, 0), ln_spec(l, 0),
                mod_spec(l, 4, tp, lat_tiles_p), mod_spec(l, 3, tp, lat_tiles_p),
            ],
            out_specs=[pl.BlockSpec((tp, d), lambda i: (i, 0)),
                       pl.BlockSpec((tp, d), lambda i: (i, 0))],
            out_shape=[jax.ShapeDtypeStruct((rows_out, d), F32),
                       jax.ShapeDtypeStruct((rows_out, d), BF16)],
            compiler_params=_cparams("parallel"),
            name=f"outproj_ln{l}",
        )(ya, yf, w_out_a, w_out_f, *x_args, mods, ln_g4, ln_b4, mods, mods)

        tn = (2 if d_ff % (2 * UP_GROUP) == 0 else 1) * UP_GROUP
        n_colt = d_ff // tn

        def upconv(period, n_tiles, tile0, prev):
            args = [h, w_up_b, conv_w3, conv_b3]
            in_specs = [
                pl.BlockSpec((t, d), lambda i, j: (tile0 + i, 0)),
                pl.BlockSpec((None, d, 2 * tn), lambda i, j: (l, 0, j)),
                pl.BlockSpec((None, 3, 2 * tn), lambda i, j: (l, 0, j)),
                pl.BlockSpec((None, 1, 2 * tn), lambda i, j: (l, 0, j)),
            ]
            kern = functools.partial(_upconv_kernel, period)
            aliases = {}
            if prev is not None:
                args.append(prev)
                in_specs.append(pl.BlockSpec(memory_space=pl.ANY))
                aliases = {len(args) - 1: 0}
                kern_inner = kern
                kern = lambda *r: kern_inner(*r[:4], r[5])
            return pl.pallas_call(
                kern,
                grid=(n_tiles, n_colt),
                in_specs=in_specs,
                out_specs=pl.BlockSpec((t, tn), lambda i, j: (tile0 + i, j)),
                out_shape=jax.ShapeDtypeStruct((rows_out, d_ff), BF16),
                input_output_aliases=aliases,
                compiler_params=_cparams("parallel", "arbitrary"),
                name=f"upconv{l}_{period}",
            )(*args)

        act = upconv(t, bsz, 0, None)
        if not last:
            act = upconv(n_ctx, n_cx // t, bsz, act)

        tk = 512
        td = tm
        lat_tiles_d = n_lat // td
        in_specs = [
            pl.BlockSpec((td, tk), lambda i, k: (i, k)),
            pl.BlockSpec((None, tk, d), lambda i, k: (l, k, 0)),
            pl.BlockSpec((td, d), lambda i, k: (i, 0)),
            mod_spec(l, 5, td, lat_tiles_d, 2), ln_spec(l, 1, 2), ln_spec(l, 1, 2),
        ]
        args = [act, w_down_b, xs, mods, ln_g4, ln_b4]
        out_specs = [pl.BlockSpec((td, d), lambda i, k: (i, 0))]
        out_shape = [jax.ShapeDtypeStruct((rows_out, d), F32)]
        if not last:
            in_specs += [mod_spec(l + 1, 1, td, lat_tiles_d, 2), mod_spec(l + 1, 0, td, lat_tiles_d, 2)]
            args += [mods, mods]
            out_specs.append(pl.BlockSpec((td, d), lambda i, k: (i, 0)))
            out_shape.append(jax.ShapeDtypeStruct((rows_out, d), BF16))
        res = pl.pallas_call(
            functools.partial(_down_ln_kernel, alpha, not last),
            grid=(rows_out // td, d_ff // tk),
            in_specs=in_specs,
            out_specs=out_specs,
            out_shape=out_shape,
            scratch_shapes=[pltpu.VMEM((td, d), F32)],
            compiler_params=_cparams("parallel", "arbitrary"),
            name=f"down_ln{l}",
        )(*args)
        if last:
            xs = res[0]
        else:
            xs, h = res

    return xs.reshape(bsz, t, d)
```

```python
import functools

import numpy as np
import jax
import jax.numpy as jnp
from jax import lax
from jax.experimental import pallas as pl
from jax.experimental.pallas import tpu as pltpu

F32 = jnp.float32
BF16 = jnp.bfloat16

HEAD_DIM = 128
FOURIER_GROUP_DIM = 128
GQA_GROUP = 3
WINDOW = 128
GRID_W = 64
ROPE_BASE = 10000.0
N_MOD = 6
LN_EPS = 1e-5
NEG_INF = -1e30
LOG2E = 1.4426950408889634
SUBLANES = 8

VMEM_LIMIT_BYTES = 56 * 1024 * 1024


def _cparams(*sem):
    return pltpu.CompilerParams(dimension_semantics=sem, vmem_limit_bytes=VMEM_LIMIT_BYTES)


def _dot(a, b):
    return jnp.dot(a, b, preferred_element_type=F32)


def _dot_nt(a, b):
    return lax.dot_general(a, b, (((1,), (1,)), ((), ())), preferred_element_type=F32)


def _silu(v):
    return v * jax.nn.sigmoid(v)


def _ada_kernel(c_ref, w_ref, b_ref, o_ref):
    s = _silu(c_ref[...]).astype(BF16)
    o_ref[...] = _dot(s, w_ref[...].astype(BF16)) + b_ref[...]


def _ada_mods(c_all, w_ada, b_ada, tn):
    n_layers, d, nd = w_ada.shape
    r = c_all.shape[0]
    return pl.pallas_call(
        _ada_kernel,
        grid=(n_layers, nd // tn),
        in_specs=[
            pl.BlockSpec((r, d), lambda l, j: (0, 0)),
            pl.BlockSpec((None, d, tn), lambda l, j: (l, 0, j)),
            pl.BlockSpec((None, 1, tn), lambda l, j: (l, 0, j)),
        ],
        out_specs=pl.BlockSpec((None, r, tn), lambda l, j: (l, 0, j)),
        out_shape=jax.ShapeDtypeStruct((n_layers, r, nd), F32),
        compiler_params=_cparams("parallel", "parallel"),
        name="ada_mods",
    )(c_all, w_ada, b_ada.reshape(n_layers, 1, nd))


def _modulate_kernel(lat_tiles, xl_ref, xc_ref, sc_ref, sh_ref, o_ref):
    x = jnp.where(pl.program_id(0) < lat_tiles, xl_ref[...], xc_ref[...])
    o_ref[...] = (x * (1.0 + sc_ref[...]) + sh_ref[...]).astype(BF16)


def _inproj_kernel(n_rope_slabs, cw, h_ref, w_ref, cos_ref, sa_ref, sb_ref, o_ref):
    h = h_ref[...]
    n_cols = w_ref.shape[1]
    slabs_per_chunk = cw // HEAD_DIM
    for jc in range(n_cols // cw):
        acc = _dot(h, w_ref[:, jc * cw:(jc + 1) * cw])
        for s in range(slabs_per_chunk):
            v = acc[:, s * HEAD_DIM:(s + 1) * HEAD_DIM]
            if jc * slabs_per_chunk + s < n_rope_slabs:
                v = (v * cos_ref[...]
                     + pltpu.roll(v, HEAD_DIM // 4, 1) * sa_ref[...]
                     + pltpu.roll(v, 3 * HEAD_DIM // 4, 1) * sb_ref[...])
            c0 = jc * cw + s * HEAD_DIM
            o_ref[:, c0:c0 + HEAD_DIM] = v.astype(BF16)


def _fourier_kernel(t, n_groups, rt, scale, f_ref, c_ref, s_ref, dft_ref, wf_ref, o_ref, w1_ref):
    for g in range(n_groups):
        cols = slice(g * FOURIER_GROUP_DIM, (g + 1) * FOURIER_GROUP_DIM)
        z = f_ref[:, cols]
        w1_ref[0:t, cols] = _dot(z, c_ref[...]).astype(BF16)
        w1_ref[t:2 * t, cols] = _dot(z, s_ref[...]).astype(BF16)
    for r in range(t // rt):
        rows = slice(r * rt, (r + 1) * rt)
        y = _dot(dft_ref[rows, :], w1_ref[...]) * scale
        o_ref[rows, :] = _dot(y.astype(BF16), wf_ref[...]).astype(BF16)


def _attn_kernel(t, tq, n_ctx, local, scale, sink_ref, q_ref, *refs):
    if local:
        k_ref, v_ref, kc_ref, vc_ref, bias_ref, o_ref = refs
    else:
        kc_ref, vc_ref, o_ref = refs
    hkv = pl.program_id(1)
    span = 3 * tq
    n_blocks = t // tq
    kc = kc_ref[...]
    vc = vc_ref[...]
    sink_col = jnp.concatenate(
        [jnp.full((tq, 1), sink_ref[hkv * GQA_GROUP + g], F32) for g in range(GQA_GROUP)], axis=0)

    def scores(j):
        q0 = pl.multiple_of(j * tq, tq)
        qb = q_ref[pl.ds(q0, tq), :]
        q3 = jnp.concatenate(
            [qb[:, g * HEAD_DIM:(g + 1) * HEAD_DIM] for g in range(GQA_GROUP)], axis=0)
        s = _dot_nt(q3, kc)
        v_all = vc
        if local:
            start = pl.multiple_of(jnp.clip((j - 1) * tq, 0, t - span), tq)
            ks = k_ref[pl.ds(start, span), :]
            bias1 = bias_ref[jnp.where(j == 0, 0, jnp.where(j == n_blocks - 1, 2, 1))]
            bias = jnp.concatenate([bias1] * GQA_GROUP, axis=0)
            s = jnp.concatenate([_dot_nt(q3, ks) + bias, s], axis=1)
            v_all = jnp.concatenate([v_ref[pl.ds(start, span), :], vc], axis=0)
        return q0, s, v_all

    def softmax_weights(s):
        m = jnp.maximum(jnp.max(s, axis=-1, keepdims=True) * scale, sink_col)
        e = jnp.exp2(s * (scale * LOG2E) - m * LOG2E)
        den = jnp.sum(e, axis=-1, keepdims=True) + jnp.exp2((sink_col - m) * LOG2E)
        return e.astype(BF16), den

    def finish(q0, e, den, v_all):
        o = _dot(e, v_all) / den
        ob = jnp.concatenate([o[g * tq:(g + 1) * tq] for g in range(GQA_GROUP)], axis=1)
        o_ref[pl.ds(q0, tq), :] = ob.astype(BF16)

    group = next(g for g in (4, 2, 1) if n_blocks % g == 0)

    def block_group(i, carry):
        staged = [scores(i * group + b) for b in range(group)]
        weights = [softmax_weights(s) for _, s, _ in staged]
        for (q0, _, v_all), (e, den) in zip(staged, weights):
            finish(q0, e, den, v_all)
        return carry

    lax.fori_loop(0, n_blocks // group, block_group, 0)


def _band_bias(t, tq):
    qi = np.arange(tq)[:, None]
    kj = np.arange(3 * tq)[None, :]
    offs = (0, -tq, -2 * tq)
    masks = [np.where(np.abs(kj + o - qi) <= WINDOW, 0.0, NEG_INF) for o in offs]
    return jnp.asarray(np.stack(masks).astype(np.float32))


LN_ROW_CHUNK = 128


def _ln_rows(alpha, y, x, gate_ref, lng_ref, lnb_ref, mod_refs, out_refs, rows):
    z = alpha * x + gate_ref[...] * y
    mu = jnp.mean(z, axis=-1, keepdims=True)
    zc = z - mu
    var = jnp.mean(zc * zc, axis=-1, keepdims=True)
    xn = zc * lax.rsqrt(var + LN_EPS) * lng_ref[...] + lnb_ref[...]
    out_refs[0][rows, :] = xn
    if mod_refs:
        sc_ref, sh_ref = mod_refs
        out_refs[1][rows, :] = (xn * (1.0 + sc_ref[...]) + sh_ref[...]).astype(BF16)


def _outproj_ln_kernel(alpha, n_y, n_x, lat_tiles, *refs):
    is_latent = pl.program_id(0) < lat_tiles

    def pick(rs):
        return rs[0][...] if len(rs) == 1 else jnp.where(is_latent, rs[0][...], rs[1][...])

    ya_refs, yf_refs, refs = refs[:n_y], refs[n_y:2 * n_y], refs[2 * n_y:]
    wa_ref, wf_ref, refs = refs[0], refs[1], refs[2:]
    x_refs, (gate_ref, lng_ref, lnb_ref, sc_ref, sh_ref, xo_ref, ho_ref) = refs[:n_x], refs[n_x:]
    x = pick(x_refs)
    y = _dot(pick(ya_refs), wa_ref[...]) + _dot(pick(yf_refs), wf_ref[...])
    _ln_rows(alpha, y, x, gate_ref, lng_ref, lnb_ref, (sc_ref, sh_ref), (xo_ref, ho_ref),
             slice(None))


def _down_ln_kernel(alpha, with_h, a_ref, w_ref, x_ref, gate_ref, lng_ref, lnb_ref, *rest):
    acc_ref = rest[-1]
    rest = rest[:-1]
    mod_refs, out_refs = (rest[:2], rest[2:]) if with_h else ((), rest)
    k = pl.program_id(1)

    @pl.when(k == 0)
    def _():
        acc_ref[...] = _dot(a_ref[...], w_ref[...])

    @pl.when(k > 0)
    def _():
        acc_ref[...] += _dot(a_ref[...], w_ref[...])

    @pl.when(k == pl.num_programs(1) - 1)
    def _():
        rc = min(LN_ROW_CHUNK, acc_ref.shape[0])

        def chunk(r, carry):
            rows = pl.ds(pl.multiple_of(r * rc, rc), rc)
            _ln_rows(alpha, acc_ref[rows, :], x_ref[rows, :], gate_ref, lng_ref, lnb_ref, mod_refs,
                     out_refs, rows)
            return carry

        lax.fori_loop(0, acc_ref.shape[0] // rc, chunk, 0)


UP_GROUP = 256


def _interleave_value_gate(a, d_ff):
    n = d_ff // UP_GROUP
    parts = [a[..., (p // 2 + (p % 2) * n) * UP_GROUP:(p // 2 + (p % 2) * n + 1) * UP_GROUP]
             for p in range(2 * n)]
    return jnp.concatenate(parts, axis=-1)


def _cast_kernel(w_ref, o_ref):
    o_ref[...] = w_ref[...].astype(BF16)


def _cast_permute_cols(w, block, src_block, name):
    n_layers, k, n = w.shape
    return pl.pallas_call(
        _cast_kernel,
        grid=(n_layers, n // block),
        in_specs=[pl.BlockSpec((None, k, block), lambda l, p: (l, 0, src_block(p)))],
        out_specs=pl.BlockSpec((None, k, block), lambda l, p: (l, 0, p)),
        out_shape=jax.ShapeDtypeStruct(w.shape, BF16),
        compiler_params=_cparams("parallel", "parallel"),
        name=name,
    )(w)


def _upconv_kernel(n_seq_tiles, short, x_ref, w_ref, cw_ref, cb_ref, o_ref):
    x = x_ref[...]
    rows = x.shape[0]
    gw = 2 * UP_GROUP
    sub = lax.broadcasted_iota(jnp.int32, (SUBLANES, gw), 0)
    short_seqs = pl.program_id(0) >= n_seq_tiles
    for c in range(w_ref.shape[1] // gw):
        cols = slice(c * gw, (c + 1) * gw)
        ocols = slice(c * UP_GROUP, (c + 1) * UP_GROUP)
        u = _dot(x, w_ref[:, cols])
        cw0, cw1, cw2, cb = cw_ref[0:1, cols], cw_ref[1:2, cols], cw_ref[2:3, cols], cb_ref[:, cols]

        def gated(prev, cur, nxt):
            v = prev * cw0 + cur * cw1 + nxt * cw2 + cb
            return (_silu(v[:, UP_GROUP:]) * v[:, :UP_GROUP]).astype(BF16)

        up, dn = pltpu.roll(u, 1, 0), pltpu.roll(u, rows - 1, 0)
        o_ref[:, ocols] = gated(up, u, dn)
        for b in range(0, rows + 1, short):
            if b < rows:
                head = slice(b, b + SUBLANES)
                before = 0.0 if b == 0 else jnp.where(
                    short_seqs, 0.0, pltpu.roll(u[b - SUBLANES:b], 1, 0))
                up8 = jnp.where(sub == 0, before, pltpu.roll(u[head], 1, 0))
                o_ref[head, ocols] = gated(up8, u[head], dn[head])
            if b > 0:
                tail = slice(b - SUBLANES, b)
                after = 0.0 if b == rows else jnp.where(
                    short_seqs, 0.0, pltpu.roll(u[b:b + SUBLANES], SUBLANES - 1, 0))
                dn8 = jnp.where(sub == SUBLANES - 1, after, pltpu.roll(u[tail], SUBLANES - 1, 0))
                o_ref[tail, ocols] = gated(up[tail], u[tail], dn8)


def _rope_tables(t, n_ident):
    rows = t // GRID_W
    row = jnp.repeat(jnp.arange(rows, dtype=F32), GRID_W)
    col = jnp.tile(jnp.arange(GRID_W, dtype=F32), rows)
    axis_dim = HEAD_DIM // 2
    inv = ROPE_BASE ** (-jnp.arange(0, axis_dim, 2, dtype=F32) / axis_dim)
    ang = jnp.concatenate([row[:, None] * inv, row[:, None] * inv,
                           col[:, None] * inv, col[:, None] * inv], axis=-1)
    lane = np.arange(HEAD_DIM)
    second_half = jnp.asarray((lane // (axis_dim // 2)) % 2 == 1)
    cos, sin = jnp.cos(ang), jnp.sin(ang)
    sa = jnp.where(second_half, sin, 0.0)
    sb = jnp.where(second_half, 0.0, -sin)
    pad = lambda a, v: jnp.concatenate([a, jnp.full((n_ident, HEAD_DIM), v, F32)], axis=0)
    return pad(cos, 1.0), pad(sa, 0.0), pad(sb, 0.0)


def _dft_constants(t):
    def cs(n):
        k = np.arange(n, dtype=np.int64)
        ph = 2.0 * np.pi * ((k[:, None] * k[None, :]) % n).astype(np.float64) / n
        return np.cos(ph), np.sin(ph)
    ct, st = cs(t)
    cg, sg = cs(FOURIER_GROUP_DIM)
    to_bf16 = lambda a: jnp.asarray(a.astype(np.float32)).astype(BF16)
    return to_bf16(np.concatenate([ct, -st], axis=1)), to_bf16(cg), to_bf16(sg)


def kernel(x, c, ctx, c_ctx, w_ada, b_ada, w_in, sink, w_four, w_out, ln_g, ln_b, w_up, conv_w,
           conv_b, w_down):
    bsz, t, d = x.shape
    n_ctx = ctx.shape[1]
    n_layers = w_ada.shape[0]
    d_f = w_four.shape[1]
    n_groups = d_f // FOURIER_GROUP_DIM
    d_q = d - d_f
    n_q = d_q // HEAD_DIM
    n_kv = n_q // GQA_GROUP
    d_kv = n_kv * HEAD_DIM
    d_in = d_f + d_q + 2 * d_kv
    d_ff = w_down.shape[1]
    alpha = (2 * n_layers) ** 0.25
    scale = HEAD_DIM ** -0.5

    n_lat, n_cx = bsz * t, bsz * n_ctx
    n_tok = n_lat + n_cx
    tm = min(1024, t, n_cx)
    tq = WINDOW
    assert t % tm == 0 and n_cx % tm == 0
    assert t % n_ctx == 0 and n_cx % t == 0 and t >= 3 * tq and t % GRID_W == 0
    assert w_in.shape[2] == d_in and d_ff % 256 == 0
    tiles_per_seq = t // tm
    lat_tiles = n_lat // tm
    cx_blk0 = n_lat // n_ctx

    n_in_blocks = d_in // d_f
    w_in_b = _cast_permute_cols(w_in, d_f, lambda p: (p + 1) % n_in_blocks, "cast_w_in")
    w_out_f = w_out[:, :d_f].astype(BF16)
    w_out_a = w_out[:, d_f:].astype(BF16)
    w_four_b = w_four.astype(BF16)
    n_up_groups = d_ff // UP_GROUP
    w_up_b = _cast_permute_cols(w_up, UP_GROUP, lambda p: p // 2 + (p % 2) * n_up_groups, "cast_w_up")
    w_down_b = w_down.astype(BF16)
    ln_g4 = ln_g.reshape(n_layers, 2, 1, d)
    ln_b4 = ln_b.reshape(n_layers, 2, 1, d)
    conv_w3 = _interleave_value_gate(conv_w, d_ff)
    conv_b3 = _interleave_value_gate(conv_b, d_ff).reshape(n_layers, 1, 2 * d_ff)

    cos_t, sa_t, sb_t = _rope_tables(t, tm)
    dft_lat = _dft_constants(t)
    dft_cx = _dft_constants(n_ctx)
    band_bias = _band_bias(t, tq)

    n_mod_rows = -(-(bsz + 1) // SUBLANES) * SUBLANES
    c_all = jnp.concatenate(
        [c, c_ctx[None, :], jnp.zeros((n_mod_rows - bsz - 1, d), F32)], axis=0)
    mods = _ada_mods(c_all, w_ada, b_ada, tn=min(1024, d)).reshape(n_layers, n_mod_rows, 1, N_MOD * d)

    def mod_row(i, tile, n_lat_tiles):
        return jnp.where(i < n_lat_tiles, i // (t // tile), bsz)

    def mod_spec(l, chunk, tile, n_lat_tiles, grid_rank=1):
        if grid_rank == 1:
            return pl.BlockSpec((None, None, 1, d),
                                lambda i: (l, mod_row(i, tile, n_lat_tiles), 0, chunk))
        return pl.BlockSpec((None, None, 1, d),
                            lambda i, k: (l, mod_row(i, tile, n_lat_tiles), 0, chunk))

    def ln_spec(l, which, grid_rank=1):
        if grid_rank == 1:
            return pl.BlockSpec((None, None, 1, d), lambda i: (l, which, 0, 0))
        return pl.BlockSpec((None, None, 1, d), lambda i, k: (l, which, 0, 0))

    x_lat, x_cx = x.reshape(n_lat, d), ctx.reshape(n_cx, d)

    def two_source_specs(tile, width=d):
        n_lat_tiles = n_lat // tile
        return [pl.BlockSpec((tile, width), lambda i: (jnp.minimum(i, n_lat_tiles - 1), 0)),
                pl.BlockSpec((tile, width), lambda i: (jnp.maximum(i - n_lat_tiles, 0), 0))]

    h = pl.pallas_call(
        functools.partial(_modulate_kernel, lat_tiles),
        grid=(n_tok // tm,),
        in_specs=two_source_specs(tm) + [mod_spec(0, 1, tm, lat_tiles), mod_spec(0, 0, tm, lat_tiles)],
        out_specs=pl.BlockSpec((tm, d), lambda i: (i, 0)),
        out_shape=jax.ShapeDtypeStruct((n_tok, d), BF16),
        compiler_params=_cparams("parallel"),
        name="modulate0",
    )(x_lat, x_cx, mods, mods)
    xs = None

    for l in range(n_layers):
        last = l == n_layers - 1
        rows_out = n_lat if last else n_tok

        qkvf = pl.pallas_call(
            functools.partial(_inproj_kernel, (d_q + d_kv) // HEAD_DIM,
                              next(w for w in (512, 256, 128) if d_in % w == 0)),
            grid=(n_tok // tm,),
            in_specs=[
                pl.BlockSpec((tm, d), lambda i: (i, 0)),
                pl.BlockSpec((None, d, d_in), lambda i: (l, 0, 0), pipeline_mode=pl.Buffered(1)),
            ] + [pl.BlockSpec((tm, HEAD_DIM),
                              lambda i: (jnp.where(i < lat_tiles, i % tiles_per_seq, tiles_per_seq), 0))
                 ] * 3,
            out_specs=pl.BlockSpec((tm, d_in), lambda i: (i, 0)),
            out_shape=jax.ShapeDtypeStruct((n_tok, d_in), BF16),
            compiler_params=_cparams("parallel"),
            name=f"inproj{l}",
        )(h, w_in_b, cos_t, sa_t, sb_t)

        f_col = (d_q + 2 * d_kv) // d_f

        def fourier(seq, n_seq, blk0, consts):
            dft, cg, sg = consts
            return pl.pallas_call(
                functools.partial(_fourier_kernel, seq, n_groups, min(512, seq),
                                  1.0 / float(np.sqrt(seq * FOURIER_GROUP_DIM))),
                grid=(n_seq,),
                in_specs=[
                    pl.BlockSpec((seq, d_f), lambda b: (blk0 + b, f_col)),
                    pl.BlockSpec((FOURIER_GROUP_DIM, FOURIER_GROUP_DIM), lambda b: (0, 0)),
                    pl.BlockSpec((FOURIER_GROUP_DIM, FOURIER_GROUP_DIM), lambda b: (0, 0)),
                    pl.BlockSpec((seq, 2 * seq), lambda b: (0, 0), pipeline_mode=pl.Buffered(1)),
                    pl.BlockSpec((None, d_f, d_f), lambda b: (l, 0, 0)),
                ],
                out_specs=pl.BlockSpec((seq, d_f), lambda b: (b, 0)),
                out_shape=jax.ShapeDtypeStruct((n_seq * seq, d_f), BF16),
                scratch_shapes=[pltpu.VMEM((2 * seq, d_f), BF16)],
                compiler_params=_cparams("parallel"),
                name=f"fourier{l}_{seq}",
            )(qkvf, cg, sg, dft, w_four_b)

        yf = [fourier(t, bsz, 0, dft_lat)]
        if not last:
            yf.append(fourier(n_ctx, bsz, cx_blk0, dft_cx))

        kc_spec = pl.BlockSpec((n_ctx, HEAD_DIM), lambda b, hh: (cx_blk0 + b, n_q + hh))
        vc_spec = pl.BlockSpec((n_ctx, HEAD_DIM), lambda b, hh: (cx_blk0 + b, n_q + n_kv + hh))
        smem_spec = pl.BlockSpec(memory_space=pltpu.SMEM)
        ya = [pl.pallas_call(
            functools.partial(_attn_kernel, t, tq, n_ctx, True, scale),
            grid=(bsz, n_kv),
            in_specs=[
                smem_spec,
                pl.BlockSpec((t, GQA_GROUP * HEAD_DIM), lambda b, hh: (b, hh)),
                pl.BlockSpec((t, HEAD_DIM), lambda b, hh: (b, n_q + hh)),
                pl.BlockSpec((t, HEAD_DIM), lambda b, hh: (b, n_q + n_kv + hh)),
                kc_spec, vc_spec,
                pl.BlockSpec((3, tq, 3 * tq), lambda b, hh: (0, 0, 0)),
            ],
            out_specs=pl.BlockSpec((t, GQA_GROUP * HEAD_DIM), lambda b, hh: (b, hh)),
            out_shape=jax.ShapeDtypeStruct((n_lat, d_q), BF16),
            compiler_params=_cparams("parallel", "parallel"),
            name=f"attn{l}",
        )(sink[l], qkvf, qkvf, qkvf, qkvf, qkvf, band_bias)]
        if not last:
            ya.append(pl.pallas_call(
                functools.partial(_attn_kernel, n_ctx, min(tq, n_ctx), n_ctx, False, scale),
                grid=(bsz, n_kv),
                in_specs=[
                    smem_spec,
                    pl.BlockSpec((n_ctx, GQA_GROUP * HEAD_DIM), lambda b, hh: (cx_blk0 + b, hh)),
                    kc_spec, vc_spec,
                ],
                out_specs=pl.BlockSpec((n_ctx, GQA_GROUP * HEAD_DIM), lambda b, hh: (b, hh)),
                out_shape=jax.ShapeDtypeStruct((n_cx, d_q), BF16),
                compiler_params=_cparams("parallel", "parallel"),
                name=f"attn_ctx{l}",
            )(sink[l], qkvf, qkvf, qkvf))

        tp = min(512, tm)
        lat_tiles_p = n_lat // tp

        def row_specs(arrays, width):
            if len(arrays) == 2:
                return two_source_specs(tp, width)
            return [pl.BlockSpec((tp, width), lambda i: (i, 0))]

        x_args = [x_lat, x_cx] if xs is None else [xs]
        xs, h = pl.pallas_call(
            functools.partial(_outproj_ln_kernel, alpha, len(ya), len(x_args), lat_tiles_p),
            grid=(rows_out // tp,),
            in_specs=row_specs(ya, d_q) + row_specs(yf, d_f) + [
                pl.BlockSpec((None, d_q, d), lambda i: (l, 0, 0), pipeline_mode=pl.Buffered(1)),
                pl.BlockSpec((None, d_f, d), lambda i: (l, 0, 0), pipeline_mode=pl.Buffered(1)),
            ] + row_specs(x_args, d) + [
                mod_spec(l, 2, tp, lat_tiles_p), ln_spec(l, 0), ln_spec(l, 0),
                mod_spec(l, 4, tp, lat_tiles_p), mod_spec(l, 3, tp, lat_tiles_p),
            ],
            out_specs=[pl.BlockSpec((tp, d), lambda i: (i, 0)),
                       pl.BlockSpec((tp, d), lambda i: (i, 0))],
            out_shape=[jax.ShapeDtypeStruct((rows_out, d), F32),
                       jax.ShapeDtypeStruct((rows_out, d), BF16)],
            compiler_params=_cparams("parallel"),
            name=f"outproj_ln{l}",
        )(*ya, *yf, w_out_a, w_out_f, *x_args, mods, ln_g4, ln_b4, mods, mods)

        tn = (2 if d_ff % (2 * UP_GROUP) == 0 else 1) * UP_GROUP
        act = pl.pallas_call(
            functools.partial(_upconv_kernel, bsz, n_ctx),
            grid=(rows_out // t, d_ff // tn),
            in_specs=[
                pl.BlockSpec((t, d), lambda i, j: (i, 0)),
                pl.BlockSpec((None, d, 2 * tn), lambda i, j: (l, 0, j)),
                pl.BlockSpec((None, 3, 2 * tn), lambda i, j: (l, 0, j)),
                pl.BlockSpec((None, 1, 2 * tn), lambda i, j: (l, 0, j)),
            ],
            out_specs=pl.BlockSpec((t, tn), lambda i, j: (i, j)),
            out_shape=jax.ShapeDtypeStruct((rows_out, d_ff), BF16),
            compiler_params=_cparams("parallel", "arbitrary"),
            name=f"upconv{l}",
        )(h, w_up_b, conv_w3, conv_b3)

        tk = 512
        td = tm
        lat_tiles_d = n_lat // td
        in_specs = [
            pl.BlockSpec((td, tk), lambda i, k: (i, k)),
            pl.BlockSpec((None, tk, d), lambda i, k: (l, k, 0)),
            pl.BlockSpec((td, d), lambda i, k: (i, 0)),
            mod_spec(l, 5, td, lat_tiles_d, 2), ln_spec(l, 1, 2), ln_spec(l, 1, 2),
        ]
        args = [act, w_down_b, xs, mods, ln_g4, ln_b4]
        out_specs = [pl.BlockSpec((td, d), lambda i, k: (i, 0))]
        out_shape = [jax.ShapeDtypeStruct((rows_out, d), F32)]
        if not last:
            in_specs += [mod_spec(l + 1, 1, td, lat_tiles_d, 2), mod_spec(l + 1, 0, td, lat_tiles_d, 2)]
            args += [mods, mods]
            out_specs.append(pl.BlockSpec((td, d), lambda i, k: (i, 0)))
            out_shape.append(jax.ShapeDtypeStruct((rows_out, d), BF16))
        res = pl.pallas_call(
            functools.partial(_down_ln_kernel, alpha, not last),
            grid=(rows_out // td, d_ff // tk),
            in_specs=in_specs,
            out_specs=out_specs,
            out_shape=out_shape,
            scratch_shapes=[pltpu.VMEM((td, d), F32)],
            compiler_params=_cparams("parallel", "arbitrary"),
            name=f"down_ln{l}",
        )(*args)
        if last:
            xs = res[0]
        else:
            xs, h = res

    return xs.reshape(bsz, t, d)
```

```python
import functools

import numpy as np
import jax
import jax.numpy as jnp
from jax import lax
from jax.experimental import pallas as pl
from jax.experimental.pallas import tpu as pltpu

F32 = jnp.float32
BF16 = jnp.bfloat16

HEAD_DIM = 128
FOURIER_GROUP_DIM = 128
GQA_GROUP = 3
WINDOW = 128
GRID_W = 64
ROPE_BASE = 10000.0
N_MOD = 6
LN_EPS = 1e-5
NEG_INF = -1e30
LOG2E = 1.4426950408889634
SUBLANES = 8

VMEM_LIMIT_BYTES = 56 * 1024 * 1024
DOWN_VMEM_LIMIT_BYTES = 60 * 1024 * 1024


def _cparams(*sem, vmem_limit_bytes=VMEM_LIMIT_BYTES):
    return pltpu.CompilerParams(dimension_semantics=sem, vmem_limit_bytes=vmem_limit_bytes)


def _dot(a, b):
    return jnp.dot(a, b, preferred_element_type=F32)


def _dot_nt(a, b):
    return lax.dot_general(a, b, (((1,), (1,)), ((), ())), preferred_element_type=F32)


def _silu(v):
    return v * jax.nn.sigmoid(v)


def _ada_kernel(c_ref, w_ref, b_ref, o_ref):
    s = _silu(c_ref[...]).astype(BF16)
    o_ref[...] = _dot(s, w_ref[...].astype(BF16)) + b_ref[...]


def _ada_mods(c_all, w_ada, b_ada, tn):
    n_layers, d, nd = w_ada.shape
    r = c_all.shape[0]
    return pl.pallas_call(
        _ada_kernel,
        grid=(n_layers, nd // tn),
        in_specs=[
            pl.BlockSpec((r, d), lambda l, j: (0, 0)),
            pl.BlockSpec((None, d, tn), lambda l, j: (l, 0, j)),
            pl.BlockSpec((None, 1, tn), lambda l, j: (l, 0, j)),
        ],
        out_specs=pl.BlockSpec((None, r, tn), lambda l, j: (l, 0, j)),
        out_shape=jax.ShapeDtypeStruct((n_layers, r, nd), F32),
        compiler_params=_cparams("parallel", "parallel"),
        name="ada_mods",
    )(c_all, w_ada, b_ada.reshape(n_layers, 1, nd))


def _modulate_kernel(lat_tiles, xl_ref, xc_ref, sc_ref, sh_ref, o_ref):
    x = jnp.where(pl.program_id(0) < lat_tiles, xl_ref[...], xc_ref[...])
    o_ref[...] = (x * (1.0 + sc_ref[...]) + sh_ref[...]).astype(BF16)


def _inproj_kernel(n_rope_slabs, cw, h_ref, w_ref, cos_ref, sa_ref, sb_ref, o_ref):
    h = h_ref[...]
    n_cols = w_ref.shape[1]
    slabs_per_chunk = cw // HEAD_DIM
    for jc in range(n_cols // cw):
        acc = _dot(h, w_ref[:, jc * cw:(jc + 1) * cw])
        for s in range(slabs_per_chunk):
            v = acc[:, s * HEAD_DIM:(s + 1) * HEAD_DIM]
            if jc * slabs_per_chunk + s < n_rope_slabs:
                v = (v * cos_ref[...]
                     + pltpu.roll(v, HEAD_DIM // 4, 1) * sa_ref[...]
                     + pltpu.roll(v, 3 * HEAD_DIM // 4, 1) * sb_ref[...])
            c0 = jc * cw + s * HEAD_DIM
            o_ref[:, c0:c0 + HEAD_DIM] = v.astype(BF16)


def _fourier_kernel(t, n_groups, rt, scale, f_ref, c_ref, s_ref, dft_ref, wf_ref, o_ref, w1_ref):
    for g in range(n_groups):
        cols = slice(g * FOURIER_GROUP_DIM, (g + 1) * FOURIER_GROUP_DIM)
        z = f_ref[:, cols]
        w1_ref[0:t, cols] = _dot(z, c_ref[...]).astype(BF16)
        w1_ref[t:2 * t, cols] = _dot(z, s_ref[...]).astype(BF16)
    for r in range(t // rt):
        rows = slice(r * rt, (r + 1) * rt)
        y = _dot(dft_ref[rows, :], w1_ref[...]) * scale
        o_ref[rows, :] = _dot(y.astype(BF16), wf_ref[...]).astype(BF16)


def _attn_kernel(t, tq, n_ctx, local, scale, sink_ref, q_ref, *refs):
    if local:
        k_ref, v_ref, kc_ref, vc_ref, bias_ref, o_ref = refs
    else:
        kc_ref, vc_ref, o_ref = refs
    hkv = pl.program_id(1)
    span = 3 * tq
    n_blocks = t // tq
    kc = kc_ref[...]
    vc = vc_ref[...]
    sink_col = jnp.concatenate(
        [jnp.full((tq, 1), sink_ref[hkv * GQA_GROUP + g], F32) for g in range(GQA_GROUP)], axis=0)

    def scores(j):
        q0 = pl.multiple_of(j * tq, tq)
        qb = q_ref[pl.ds(q0, tq), :]
        q3 = jnp.concatenate(
            [qb[:, g * HEAD_DIM:(g + 1) * HEAD_DIM] for g in range(GQA_GROUP)], axis=0)
        s = _dot_nt(q3, kc)
        v_all = vc
        if local:
            start = pl.multiple_of(jnp.clip((j - 1) * tq, 0, t - span), tq)
            ks = k_ref[pl.ds(start, span), :]
            bias1 = bias_ref[jnp.where(j == 0, 0, jnp.where(j == n_blocks - 1, 2, 1))]
            bias = jnp.concatenate([bias1] * GQA_GROUP, axis=0)
            s = jnp.concatenate([_dot_nt(q3, ks) + bias, s], axis=1)
            v_all = jnp.concatenate([v_ref[pl.ds(start, span), :], vc], axis=0)
        return q0, s, v_all

    def softmax_weights(s):
        m = jnp.maximum(jnp.max(s, axis=-1, keepdims=True) * scale, sink_col)
        e = jnp.exp2(s * (scale * LOG2E) - m * LOG2E)
        den = jnp.sum(e, axis=-1, keepdims=True) + jnp.exp2((sink_col - m) * LOG2E)
        return e.astype(BF16), den

    def finish(q0, e, den, v_all):
        o = _dot(e, v_all) / den
        ob = jnp.concatenate([o[g * tq:(g + 1) * tq] for g in range(GQA_GROUP)], axis=1)
        o_ref[pl.ds(q0, tq), :] = ob.astype(BF16)

    group = next(g for g in (8, 4, 2, 1) if n_blocks % g == 0)

    def block_group(i, carry):
        staged = [scores(i * group + b) for b in range(group)]
        weights = [softmax_weights(s) for _, s, _ in staged]
        for (q0, _, v_all), (e, den) in zip(staged, weights):
            finish(q0, e, den, v_all)
        return carry

    lax.fori_loop(0, n_blocks // group, block_group, 0)


def _band_bias(t, tq):
    qi = np.arange(tq)[:, None]
    kj = np.arange(3 * tq)[None, :]
    offs = (0, -tq, -2 * tq)
    masks = [np.where(np.abs(kj + o - qi) <= WINDOW, 0.0, NEG_INF) for o in offs]
    return jnp.asarray(np.stack(masks).astype(np.float32))


LN_ROW_CHUNK = 128


def _ln_rows(alpha, y, x, gate_ref, lng_ref, lnb_ref, mod_refs, out_refs, rows):
    z = alpha * x + gate_ref[...] * y
    mu = jnp.mean(z, axis=-1, keepdims=True)
    zc = z - mu
    var = jnp.mean(zc * zc, axis=-1, keepdims=True)
    xn = zc * lax.rsqrt(var + LN_EPS) * lng_ref[...] + lnb_ref[...]
    out_refs[0][rows, :] = xn
    if mod_refs:
        sc_ref, sh_ref = mod_refs
        out_refs[1][rows, :] = (xn * (1.0 + sc_ref[...]) + sh_ref[...]).astype(BF16)


def _outproj_ln_kernel(alpha, n_y, n_x, lat_tiles, *refs):
    is_latent = pl.program_id(0) < lat_tiles

    def pick(rs):
        return rs[0][...] if len(rs) == 1 else jnp.where(is_latent, rs[0][...], rs[1][...])

    ya_refs, yf_refs, refs = refs[:n_y], refs[n_y:2 * n_y], refs[2 * n_y:]
    wa_ref, wf_ref, refs = refs[0], refs[1], refs[2:]
    x_refs, (gate_ref, lng_ref, lnb_ref, sc_ref, sh_ref, xo_ref, ho_ref) = refs[:n_x], refs[n_x:]
    x = pick(x_refs)
    y = _dot(pick(ya_refs), wa_ref[...]) + _dot(pick(yf_refs), wf_ref[...])
    _ln_rows(alpha, y, x, gate_ref, lng_ref, lnb_ref, (sc_ref, sh_ref), (xo_ref, ho_ref),
             slice(None))


def _down_ln_kernel(alpha, with_h, n_k, a_ref, w_ref, x_ref, gate_ref, lng_ref, lnb_ref, *rest):
    acc_ref = rest[-1]
    rest = rest[:-1]
    mod_refs, out_refs = (rest[:2], rest[2:]) if with_h else ((), rest)
    if n_k == 1:
        _ln_rows(alpha, _dot(a_ref[...], w_ref[...]), x_ref[...], gate_ref, lng_ref, lnb_ref,
                 mod_refs, out_refs, slice(None))
        return
    k = pl.program_id(1)

    @pl.when(k == 0)
    def _():
        acc_ref[...] = _dot(a_ref[...], w_ref[...])

    @pl.when(k > 0)
    def _():
        acc_ref[...] += _dot(a_ref[...], w_ref[...])

    @pl.when(k == pl.num_programs(1) - 1)
    def _():
        rc = min(LN_ROW_CHUNK, acc_ref.shape[0])

        def chunk(r, carry):
            rows = pl.ds(pl.multiple_of(r * rc, rc), rc)
            _ln_rows(alpha, acc_ref[rows, :], x_ref[rows, :], gate_ref, lng_ref, lnb_ref, mod_refs,
                     out_refs, rows)
            return carry

        lax.fori_loop(0, acc_ref.shape[0] // rc, chunk, 0)


UP_GROUP = 256


def _interleave_value_gate(a, d_ff):
    n = d_ff // UP_GROUP
    parts = [a[..., (p // 2 + (p % 2) * n) * UP_GROUP:(p // 2 + (p % 2) * n + 1) * UP_GROUP]
             for p in range(2 * n)]
    return jnp.concatenate(parts, axis=-1)


def _cast_kernel(w_ref, o_ref):
    o_ref[...] = w_ref[...].astype(BF16)


def _cast_permute_cols(w, block, src_block, name):
    n_layers, k, n = w.shape
    return pl.pallas_call(
        _cast_kernel,
        grid=(n_layers, n // block),
        in_specs=[pl.BlockSpec((None, k, block), lambda l, p: (l, 0, src_block(p)))],
        out_specs=pl.BlockSpec((None, k, block), lambda l, p: (l, 0, p)),
        out_shape=jax.ShapeDtypeStruct(w.shape, BF16),
        compiler_params=_cparams("parallel", "parallel"),
        name=name,
    )(w)


def _upconv_kernel(n_seq_tiles, short, x_ref, w_ref, cw_ref, cb_ref, o_ref):
    x = x_ref[...]
    rows = x.shape[0]
    gw = 2 * UP_GROUP
    sub = lax.broadcasted_iota(jnp.int32, (SUBLANES, gw), 0)
    short_seqs = pl.program_id(0) >= n_seq_tiles
    for c in range(w_ref.shape[1] // gw):
        cols = slice(c * gw, (c + 1) * gw)
        ocols = slice(c * UP_GROUP, (c + 1) * UP_GROUP)
        u = _dot(x, w_ref[:, cols])
        cw0, cw1, cw2, cb = cw_ref[0:1, cols], cw_ref[1:2, cols], cw_ref[2:3, cols], cb_ref[:, cols]

        def gated(prev, cur, nxt):
            v = prev * cw0 + cur * cw1 + nxt * cw2 + cb
            return (_silu(v[:, UP_GROUP:]) * v[:, :UP_GROUP]).astype(BF16)

        up, dn = pltpu.roll(u, 1, 0), pltpu.roll(u, rows - 1, 0)
        o_ref[:, ocols] = gated(up, u, dn)
        for b in range(0, rows + 1, short):
            if b < rows:
                head = slice(b, b + SUBLANES)
                before = 0.0 if b == 0 else jnp.where(
                    short_seqs, 0.0, pltpu.roll(u[b - SUBLANES:b], 1, 0))
                up8 = jnp.where(sub == 0, before, pltpu.roll(u[head], 1, 0))
                o_ref[head, ocols] = gated(up8, u[head], dn[head])
            if b > 0:
                tail = slice(b - SUBLANES, b)
                after = 0.0 if b == rows else jnp.where(
                    short_seqs, 0.0, pltpu.roll(u[b:b + SUBLANES], SUBLANES - 1, 0))
                dn8 = jnp.where(sub == SUBLANES - 1, after, pltpu.roll(u[tail], SUBLANES - 1, 0))
                o_ref[tail, ocols] = gated(up[tail], u[tail], dn8)


def _rope_tables(t, n_ident):
    rows = t // GRID_W
    row = jnp.repeat(jnp.arange(rows, dtype=F32), GRID_W)
    col = jnp.tile(jnp.arange(GRID_W, dtype=F32), rows)
    axis_dim = HEAD_DIM // 2
    inv = ROPE_BASE ** (-jnp.arange(0, axis_dim, 2, dtype=F32) / axis_dim)
    ang = jnp.concatenate([row[:, None] * inv, row[:, None] * inv,
                           col[:, None] * inv, col[:, None] * inv], axis=-1)
    lane = np.arange(HEAD_DIM)
    second_half = jnp.asarray((lane // (axis_dim // 2)) % 2 == 1)
    cos, sin = jnp.cos(ang), jnp.sin(ang)
    sa = jnp.where(second_half, sin, 0.0)
    sb = jnp.where(second_half, 0.0, -sin)
    pad = lambda a, v: jnp.concatenate([a, jnp.full((n_ident, HEAD_DIM), v, F32)], axis=0)
    return pad(cos, 1.0), pad(sa, 0.0), pad(sb, 0.0)


def _dft_constants(t):
    def cs(n):
        k = np.arange(n, dtype=np.int64)
        ph = 2.0 * np.pi * ((k[:, None] * k[None, :]) % n).astype(np.float64) / n
        return np.cos(ph), np.sin(ph)
    ct, st = cs(t)
    cg, sg = cs(FOURIER_GROUP_DIM)
    to_bf16 = lambda a: jnp.asarray(a.astype(np.float32)).astype(BF16)
    return to_bf16(np.concatenate([ct, -st], axis=1)), to_bf16(cg), to_bf16(sg)


def kernel(x, c, ctx, c_ctx, w_ada, b_ada, w_in, sink, w_four, w_out, ln_g, ln_b, w_up, conv_w,
           conv_b, w_down):
    bsz, t, d = x.shape
    n_ctx = ctx.shape[1]
    n_layers = w_ada.shape[0]
    d_f = w_four.shape[1]
    n_groups = d_f // FOURIER_GROUP_DIM
    d_q = d - d_f
    n_q = d_q // HEAD_DIM
    n_kv = n_q // GQA_GROUP
    d_kv = n_kv * HEAD_DIM
    d_in = d_f + d_q + 2 * d_kv
    d_ff = w_down.shape[1]
    alpha = (2 * n_layers) ** 0.25
    scale = HEAD_DIM ** -0.5

    n_lat, n_cx = bsz * t, bsz * n_ctx
    n_tok = n_lat + n_cx
    tm = min(1024, t, n_cx)
    tq = WINDOW
    assert t % tm == 0 and n_cx % tm == 0
    assert t % n_ctx == 0 and n_cx % t == 0 and t >= 3 * tq and t % GRID_W == 0
    assert w_in.shape[2] == d_in and d_ff % 256 == 0
    tiles_per_seq = t // tm
    lat_tiles = n_lat // tm
    cx_blk0 = n_lat // n_ctx

    n_in_blocks = d_in // d_f
    w_in_b = _cast_permute_cols(w_in, d_f, lambda p: (p + 1) % n_in_blocks, "cast_w_in")
    w_out_f = w_out[:, :d_f].astype(BF16)
    w_out_a = w_out[:, d_f:].astype(BF16)
    w_four_b = w_four.astype(BF16)
    n_up_groups = d_ff // UP_GROUP
    w_up_b = _cast_permute_cols(w_up, UP_GROUP, lambda p: p // 2 + (p % 2) * n_up_groups, "cast_w_up")
    w_down_b = w_down.astype(BF16)
    ln_g4 = ln_g.reshape(n_layers, 2, 1, d)
    ln_b4 = ln_b.reshape(n_layers, 2, 1, d)
    conv_w3 = _interleave_value_gate(conv_w, d_ff)
    conv_b3 = _interleave_value_gate(conv_b, d_ff).reshape(n_layers, 1, 2 * d_ff)

    cos_t, sa_t, sb_t = _rope_tables(t, tm)
    dft_lat = _dft_constants(t)
    dft_cx = _dft_constants(n_ctx)
    band_bias = _band_bias(t, tq)

    n_mod_rows = -(-(bsz + 1) // SUBLANES) * SUBLANES
    c_all = jnp.concatenate(
        [c, c_ctx[None, :], jnp.zeros((n_mod_rows - bsz - 1, d), F32)], axis=0)
    mods = _ada_mods(c_all, w_ada, b_ada, tn=min(1024, d)).reshape(n_layers, n_mod_rows, 1, N_MOD * d)

    def mod_row(i, tile, n_lat_tiles):
        return jnp.where(i < n_lat_tiles, i // (t // tile), bsz)

    def mod_spec(l, chunk, tile, n_lat_tiles, grid_rank=1):
        if grid_rank == 1:
            return pl.BlockSpec((None, None, 1, d),
                                lambda i: (l, mod_row(i, tile, n_lat_tiles), 0, chunk))
        return pl.BlockSpec((None, None, 1, d),
                            lambda i, k: (l, mod_row(i, tile, n_lat_tiles), 0, chunk))

    def ln_spec(l, which, grid_rank=1):
        if grid_rank == 1:
            return pl.BlockSpec((None, None, 1, d), lambda i: (l, which, 0, 0))
        return pl.BlockSpec((None, None, 1, d), lambda i, k: (l, which, 0, 0))

    x_lat, x_cx = x.reshape(n_lat, d), ctx.reshape(n_cx, d)

    def two_source_specs(tile, width=d):
        n_lat_tiles = n_lat // tile
        return [pl.BlockSpec((tile, width), lambda i: (jnp.minimum(i, n_lat_tiles - 1), 0)),
                pl.BlockSpec((tile, width), lambda i: (jnp.maximum(i - n_lat_tiles, 0), 0))]

    h = pl.pallas_call(
        functools.partial(_modulate_kernel, lat_tiles),
        grid=(n_tok // tm,),
        in_specs=two_source_specs(tm) + [mod_spec(0, 1, tm, lat_tiles), mod_spec(0, 0, tm, lat_tiles)],
        out_specs=pl.BlockSpec((tm, d), lambda i: (i, 0)),
        out_shape=jax.ShapeDtypeStruct((n_tok, d), BF16),
        compiler_params=_cparams("parallel"),
        name="modulate0",
    )(x_lat, x_cx, mods, mods)
    xs = None

    for l in range(n_layers):
        last = l == n_layers - 1
        rows_out = n_lat if last else n_tok

        qkvf = pl.pallas_call(
            functools.partial(_inproj_kernel, (d_q + d_kv) // HEAD_DIM,
                              next(w for w in (512, 256, 128) if d_in % w == 0)),
            grid=(n_tok // tm,),
            in_specs=[
                pl.BlockSpec((tm, d), lambda i: (i, 0)),
                pl.BlockSpec((None, d, d_in), lambda i: (l, 0, 0), pipeline_mode=pl.Buffered(1)),
            ] + [pl.BlockSpec((tm, HEAD_DIM),
                              lambda i: (jnp.where(i < lat_tiles, i % tiles_per_seq, tiles_per_seq), 0))
                 ] * 3,
            out_specs=pl.BlockSpec((tm, d_in), lambda i: (i, 0)),
            out_shape=jax.ShapeDtypeStruct((n_tok, d_in), BF16),
            compiler_params=_cparams("parallel"),
            name=f"inproj{l}",
        )(h, w_in_b, cos_t, sa_t, sb_t)

        f_col = (d_q + 2 * d_kv) // d_f

        def fourier(seq, n_seq, blk0, consts):
            dft, cg, sg = consts
            return pl.pallas_call(
                functools.partial(_fourier_kernel, seq, n_groups, min(512, seq),
                                  1.0 / float(np.sqrt(seq * FOURIER_GROUP_DIM))),
                grid=(n_seq,),
                in_specs=[
                    pl.BlockSpec((seq, d_f), lambda b: (blk0 + b, f_col)),
                    pl.BlockSpec((FOURIER_GROUP_DIM, FOURIER_GROUP_DIM), lambda b: (0, 0)),
                    pl.BlockSpec((FOURIER_GROUP_DIM, FOURIER_GROUP_DIM), lambda b: (0, 0)),
                    pl.BlockSpec((seq, 2 * seq), lambda b: (0, 0), pipeline_mode=pl.Buffered(1)),
                    pl.BlockSpec((None, d_f, d_f), lambda b: (l, 0, 0)),
                ],
                out_specs=pl.BlockSpec((seq, d_f), lambda b: (b, 0)),
                out_shape=jax.ShapeDtypeStruct((n_seq * seq, d_f), BF16),
                scratch_shapes=[pltpu.VMEM((2 * seq, d_f), BF16)],
                compiler_params=_cparams("parallel"),
                name=f"fourier{l}_{seq}",
            )(qkvf, cg, sg, dft, w_four_b)

        yf = [fourier(t, bsz, 0, dft_lat)]
        if not last:
            yf.append(fourier(n_ctx, bsz, cx_blk0, dft_cx))

        kc_spec = pl.BlockSpec((n_ctx, HEAD_DIM), lambda b, hh: (cx_blk0 + b, n_q + hh))
        vc_spec = pl.BlockSpec((n_ctx, HEAD_DIM), lambda b, hh: (cx_blk0 + b, n_q + n_kv + hh))
        smem_spec = pl.BlockSpec(memory_space=pltpu.SMEM)
        ya = [pl.pallas_call(
            functools.partial(_attn_kernel, t, tq, n_ctx, True, scale),
            grid=(bsz, n_kv),
            in_specs=[
                smem_spec,
                pl.BlockSpec((t, GQA_GROUP * HEAD_DIM), lambda b, hh: (b, hh)),
                pl.BlockSpec((t, HEAD_DIM), lambda b, hh: (b, n_q + hh)),
                pl.BlockSpec((t, HEAD_DIM), lambda b, hh: (b, n_q + n_kv + hh)),
                kc_spec, vc_spec,
                pl.BlockSpec((3, tq, 3 * tq), lambda b, hh: (0, 0, 0)),
            ],
            out_specs=pl.BlockSpec((t, GQA_GROUP * HEAD_DIM), lambda b, hh: (b, hh)),
            out_shape=jax.ShapeDtypeStruct((n_lat, d_q), BF16),
            compiler_params=_cparams("parallel", "parallel"),
            name=f"attn{l}",
        )(sink[l], qkvf, qkvf, qkvf, qkvf, qkvf, band_bias)]
        if not last:
            ya.append(pl.pallas_call(
                functools.partial(_attn_kernel, n_ctx, min(tq, n_ctx), n_ctx, False, scale),
                grid=(bsz, n_kv),
                in_specs=[
                    smem_spec,
                    pl.BlockSpec((n_ctx, GQA_GROUP * HEAD_DIM), lambda b, hh: (cx_blk0 + b, hh)),
                    kc_spec, vc_spec,
                ],
                out_specs=pl.BlockSpec((n_ctx, GQA_GROUP * HEAD_DIM), lambda b, hh: (b, hh)),
                out_shape=jax.ShapeDtypeStruct((n_cx, d_q), BF16),
                compiler_params=_cparams("parallel", "parallel"),
                name=f"attn_ctx{l}",
            )(sink[l], qkvf, qkvf, qkvf))

        tp = min(512, tm)
        lat_tiles_p = n_lat // tp

        def row_specs(arrays, width):
            if len(arrays) == 2:
                return two_source_specs(tp, width)
            return [pl.BlockSpec((tp, width), lambda i: (i, 0))]

        x_args = [x_lat, x_cx] if xs is None else [xs]
        xs, h = pl.pallas_call(
            functools.partial(_outproj_ln_kernel, alpha, len(ya), len(x_args), lat_tiles_p),
            grid=(rows_out // tp,),
            in_specs=row_specs(ya, d_q) + row_specs(yf, d_f) + [
                pl.BlockSpec((None, d_q, d), lambda i: (l, 0, 0), pipeline_mode=pl.Buffered(1)),
                pl.BlockSpec((None, d_f, d), lambda i: (l, 0, 0), pipeline_mode=pl.Buffered(1)),
            ] + row_specs(x_args, d) + [
                mod_spec(l, 2, tp, lat_tiles_p), ln_spec(l, 0), ln_spec(l, 0),
                mod_spec(l, 4, tp, lat_tiles_p), mod_spec(l, 3, tp, lat_tiles_p),
            ],
            out_specs=[pl.BlockSpec((tp, d), lambda i: (i, 0)),
                       pl.BlockSpec((tp, d), lambda i: (i, 0))],
            out_shape=[jax.ShapeDtypeStruct((rows_out, d), F32),
                       jax.ShapeDtypeStruct((rows_out, d), BF16)],
            compiler_params=_cparams("parallel"),
            name=f"outproj_ln{l}",
        )(*ya, *yf, w_out_a, w_out_f, *x_args, mods, ln_g4, ln_b4, mods, mods)

        tn = (2 if d_ff % (2 * UP_GROUP) == 0 else 1) * UP_GROUP
        act = pl.pallas_call(
            functools.partial(_upconv_kernel, bsz, n_ctx),
            grid=(rows_out // t, d_ff // tn),
            in_specs=[
                pl.BlockSpec((t, d), lambda i, j: (i, 0)),
                pl.BlockSpec((None, d, 2 * tn), lambda i, j: (l, 0, j)),
                pl.BlockSpec((None, 3, 2 * tn), lambda i, j: (l, 0, j)),
                pl.BlockSpec((None, 1, 2 * tn), lambda i, j: (l, 0, j)),
            ],
            out_specs=pl.BlockSpec((t, tn), lambda i, j: (i, j)),
            out_shape=jax.ShapeDtypeStruct((rows_out, d_ff), BF16),
            compiler_params=_cparams("parallel", "arbitrary"),
            name=f"upconv{l}",
        )(h, w_up_b, conv_w3, conv_b3)

        tk = d_ff
        td = tp
        lat_tiles_d = n_lat // td
        in_specs = [
            pl.BlockSpec((td, tk), lambda i, k: (i, k)),
            pl.BlockSpec((None, tk, d), lambda i, k: (l, k, 0), pipeline_mode=pl.Buffered(1)),
            pl.BlockSpec((td, d), lambda i, k: (i, 0)),
            mod_spec(l, 5, td, lat_tiles_d, 2), ln_spec(l, 1, 2), ln_spec(l, 1, 2),
        ]
        args = [act, w_down_b, xs, mods, ln_g4, ln_b4]
        out_specs = [pl.BlockSpec((td, d), lambda i, k: (i, 0))]
        out_shape = [jax.ShapeDtypeStruct((rows_out, d), F32)]
        if not last:
            in_specs += [mod_spec(l + 1, 1, td, lat_tiles_d, 2), mod_spec(l + 1, 0, td, lat_tiles_d, 2)]
            args += [mods, mods]
            out_specs.append(pl.BlockSpec((td, d), lambda i, k: (i, 0)))
            out_shape.append(jax.ShapeDtypeStruct((rows_out, d), BF16))
        res = pl.pallas_call(
            functools.partial(_down_ln_kernel, alpha, not last, d_ff // tk),
            grid=(rows_out // td, d_ff // tk),
            in_specs=in_specs,
            out_specs=out_specs,
            out_shape=out_shape,
            scratch_shapes=[pltpu.VMEM((td if d_ff // tk > 1 else SUBLANES, d), F32)],
            compiler_params=_cparams("parallel", "arbitrary", vmem_limit_bytes=DOWN_VMEM_LIMIT_BYTES),
            name=f"down_ln{l}",
        )(*args)
        if last:
            xs = res[0]
        else:
            xs, h = res

    return xs.reshape(bsz, t, d)
```

```python
import functools

import numpy as np
import jax
import jax.numpy as jnp
from jax import lax
from jax.experimental import pallas as pl
from jax.experimental.pallas import tpu as pltpu

F32 = jnp.float32
BF16 = jnp.bfloat16

HEAD_DIM = 128
FOURIER_GROUP_DIM = 128
GQA_GROUP = 3
WINDOW = 128
GRID_W = 64
ROPE_BASE = 10000.0
N_MOD = 6
LN_EPS = 1e-5
NEG_INF = -1e30
LOG2E = 1.4426950408889634
SUBLANES = 8

VMEM_LIMIT_BYTES = 56 * 1024 * 1024
DOWN_VMEM_LIMIT_BYTES = 60 * 1024 * 1024


def _cparams(*sem, vmem_limit_bytes=VMEM_LIMIT_BYTES):
    return pltpu.CompilerParams(dimension_semantics=sem, vmem_limit_bytes=vmem_limit_bytes)


def _dot(a, b):
    return jnp.dot(a, b, preferred_element_type=F32)


def _dot_nt(a, b):
    return lax.dot_general(a, b, (((1,), (1,)), ((), ())), preferred_element_type=F32)


def _silu(v):
    return v * jax.nn.sigmoid(v)


def _ada_kernel(c_ref, w_ref, b_ref, o_ref):
    s = _silu(c_ref[...]).astype(BF16)
    o_ref[...] = _dot(s, w_ref[...].astype(BF16)) + b_ref[...]


def _ada_mods(c_all, w_ada, b_ada, tn):
    n_layers, d, nd = w_ada.shape
    r = c_all.shape[0]
    return pl.pallas_call(
        _ada_kernel,
        grid=(n_layers, nd // tn),
        in_specs=[
            pl.BlockSpec((r, d), lambda l, j: (0, 0)),
            pl.BlockSpec((None, d, tn), lambda l, j: (l, 0, j)),
            pl.BlockSpec((None, 1, tn), lambda l, j: (l, 0, j)),
        ],
        out_specs=pl.BlockSpec((None, r, tn), lambda l, j: (l, 0, j)),
        out_shape=jax.ShapeDtypeStruct((n_layers, r, nd), F32),
        compiler_params=_cparams("parallel", "parallel"),
        name="ada_mods",
    )(c_all, w_ada, b_ada.reshape(n_layers, 1, nd))


def _modulate_kernel(lat_tiles, xl_ref, xc_ref, sc_ref, sh_ref, o_ref):
    x = jnp.where(pl.program_id(0) < lat_tiles, xl_ref[...], xc_ref[...])
    o_ref[...] = (x * (1.0 + sc_ref[...]) + sh_ref[...]).astype(BF16)


def _inproj_kernel(n_rope_slabs, cw, h_ref, w_ref, cos_ref, sa_ref, sb_ref, o_ref):
    h = h_ref[...]
    n_cols = w_ref.shape[1]
    slabs_per_chunk = cw // HEAD_DIM
    for jc in range(n_cols // cw):
        acc = _dot(h, w_ref[:, jc * cw:(jc + 1) * cw])
        for s in range(slabs_per_chunk):
            v = acc[:, s * HEAD_DIM:(s + 1) * HEAD_DIM]
            if jc * slabs_per_chunk + s < n_rope_slabs:
                v = (v * cos_ref[...]
                     + pltpu.roll(v, HEAD_DIM // 4, 1) * sa_ref[...]
                     + pltpu.roll(v, 3 * HEAD_DIM // 4, 1) * sb_ref[...])
            c0 = jc * cw + s * HEAD_DIM
            o_ref[:, c0:c0 + HEAD_DIM] = v.astype(BF16)


def _fourier_kernel(t, n_groups, rt, scale, f_ref, c_ref, s_ref, dft_ref, wf_ref, o_ref, w1_ref):
    for g in range(n_groups):
        cols = slice(g * FOURIER_GROUP_DIM, (g + 1) * FOURIER_GROUP_DIM)
        z = f_ref[:, cols]
        w1_ref[0:t, cols] = _dot(z, c_ref[...]).astype(BF16)
        w1_ref[t:2 * t, cols] = _dot(z, s_ref[...]).astype(BF16)
    for r in range(t // rt):
        rows = slice(r * rt, (r + 1) * rt)
        y = _dot(dft_ref[rows, :], w1_ref[...]) * scale
        o_ref[rows, :] = _dot(y.astype(BF16), wf_ref[...]).astype(BF16)


def _attn_kernel(t, tq, n_ctx, local, scale, sink_ref, q_ref, *refs):
    if local:
        k_ref, v_ref, kc_ref, vc_ref, bias_ref, o_ref = refs
    else:
        kc_ref, vc_ref, o_ref = refs
    hkv = pl.program_id(1)
    span = 3 * tq
    n_blocks = t // tq
    kc = kc_ref[...]
    vc = vc_ref[...]
    sink_col = jnp.concatenate(
        [jnp.full((tq, 1), sink_ref[hkv * GQA_GROUP + g], F32) for g in range(GQA_GROUP)], axis=0)

    def scores(j):
        q0 = pl.multiple_of(j * tq, tq)
        qb = q_ref[pl.ds(q0, tq), :]
        q3 = jnp.concatenate(
            [qb[:, g * HEAD_DIM:(g + 1) * HEAD_DIM] for g in range(GQA_GROUP)], axis=0)
        s = _dot_nt(q3, kc)
        v_all = vc
        if local:
            start = pl.multiple_of(jnp.clip((j - 1) * tq, 0, t - span), tq)
            ks = k_ref[pl.ds(start, span), :]
            bias1 = bias_ref[jnp.where(j == 0, 0, jnp.where(j == n_blocks - 1, 2, 1))]
            bias = jnp.concatenate([bias1] * GQA_GROUP, axis=0)
            s = jnp.concatenate([_dot_nt(q3, ks) + bias, s], axis=1)
            v_all = jnp.concatenate([v_ref[pl.ds(start, span), :], vc], axis=0)
        return q0, s, v_all

    def softmax_weights(s):
        m = jnp.maximum(jnp.max(s, axis=-1, keepdims=True) * scale, sink_col)
        e = jnp.exp2(s * (scale * LOG2E) - m * LOG2E)
        den = jnp.sum(e, axis=-1, keepdims=True) + jnp.exp2((sink_col - m) * LOG2E)
        return e.astype(BF16), den

    def finish(q0, e, den, v_all):
        o = _dot(e, v_all) / den
        ob = jnp.concatenate([o[g * tq:(g + 1) * tq] for g in range(GQA_GROUP)], axis=1)
        o_ref[pl.ds(q0, tq), :] = ob.astype(BF16)

    group = next(g for g in (8, 4, 2, 1) if n_blocks % g == 0)

    def block_group(i, carry):
        staged = [scores(i * group + b) for b in range(group)]
        weights = [softmax_weights(s) for _, s, _ in staged]
        for (q0, _, v_all), (e, den) in zip(staged, weights):
            finish(q0, e, den, v_all)
        return carry

    lax.fori_loop(0, n_blocks // group, block_group, 0)


def _band_bias(t, tq):
    qi = np.arange(tq)[:, None]
    kj = np.arange(3 * tq)[None, :]
    offs = (0, -tq, -2 * tq)
    masks = [np.where(np.abs(kj + o - qi) <= WINDOW, 0.0, NEG_INF) for o in offs]
    return jnp.asarray(np.stack(masks).astype(np.float32))


LN_ROW_CHUNK = 128


def _ln_rows(alpha, y, x, gate_ref, lng_ref, lnb_ref, mod_refs, out_refs, rows):
    z = alpha * x + gate_ref[...] * y
    mu = jnp.mean(z, axis=-1, keepdims=True)
    zc = z - mu
    var = jnp.mean(zc * zc, axis=-1, keepdims=True)
    xn = zc * lax.rsqrt(var + LN_EPS) * lng_ref[...] + lnb_ref[...]
    out_refs[0][rows, :] = xn
    if mod_refs:
        sc_ref, sh_ref = mod_refs
        out_refs[1][rows, :] = (xn * (1.0 + sc_ref[...]) + sh_ref[...]).astype(BF16)


def _outproj_ln_kernel(alpha, n_y, n_x, lat_tiles, *refs):
    is_latent = pl.program_id(0) < lat_tiles

    def pick(rs):
        return rs[0][...] if len(rs) == 1 else jnp.where(is_latent, rs[0][...], rs[1][...])

    ya_refs, yf_refs, refs = refs[:n_y], refs[n_y:2 * n_y], refs[2 * n_y:]
    wa_ref, wf_ref, refs = refs[0], refs[1], refs[2:]
    x_refs, (gate_ref, lng_ref, lnb_ref, sc_ref, sh_ref, xo_ref, ho_ref) = refs[:n_x], refs[n_x:]
    x = pick(x_refs)
    y = _dot(pick(ya_refs), wa_ref[...]) + _dot(pick(yf_refs), wf_ref[...])
    _ln_rows(alpha, y, x, gate_ref, lng_ref, lnb_ref, (sc_ref, sh_ref), (xo_ref, ho_ref),
             slice(None))


def _down_ln_kernel(alpha, with_h, n_k, a_ref, w_ref, x_ref, gate_ref, lng_ref, lnb_ref, *rest):
    acc_ref = rest[-1]
    rest = rest[:-1]
    mod_refs, out_refs = (rest[:2], rest[2:]) if with_h else ((), rest)
    if n_k == 1:
        _ln_rows(alpha, _dot(a_ref[...], w_ref[...]), x_ref[...], gate_ref, lng_ref, lnb_ref,
                 mod_refs, out_refs, slice(None))
        return
    k = pl.program_id(1)

    @pl.when(k == 0)
    def _():
        acc_ref[...] = _dot(a_ref[...], w_ref[...])

    @pl.when(k > 0)
    def _():
        acc_ref[...] += _dot(a_ref[...], w_ref[...])

    @pl.when(k == pl.num_programs(1) - 1)
    def _():
        rc = min(LN_ROW_CHUNK, acc_ref.shape[0])

        def chunk(r, carry):
            rows = pl.ds(pl.multiple_of(r * rc, rc), rc)
            _ln_rows(alpha, acc_ref[rows, :], x_ref[rows, :], gate_ref, lng_ref, lnb_ref, mod_refs,
                     out_refs, rows)
            return carry

        lax.fori_loop(0, acc_ref.shape[0] // rc, chunk, 0)


UP_GROUP = 256


def _interleave_value_gate(a, d_ff):
    n = d_ff // UP_GROUP
    parts = [a[..., (p // 2 + (p % 2) * n) * UP_GROUP:(p // 2 + (p % 2) * n + 1) * UP_GROUP]
             for p in range(2 * n)]
    return jnp.concatenate(parts, axis=-1)


def _cast_kernel(w_ref, o_ref):
    o_ref[...] = w_ref[...].astype(BF16)


def _cast_permute_cols(w, block, src_block, name):
    n_layers, k, n = w.shape
    return pl.pallas_call(
        _cast_kernel,
        grid=(n_layers, n // block),
        in_specs=[pl.BlockSpec((None, k, block), lambda l, p: (l, 0, src_block(p)))],
        out_specs=pl.BlockSpec((None, k, block), lambda l, p: (l, 0, p)),
        out_shape=jax.ShapeDtypeStruct(w.shape, BF16),
        compiler_params=_cparams("parallel", "parallel"),
        name=name,
    )(w)


def _upconv_kernel(n_seq_tiles, short, x_ref, w_ref, cw_ref, cb_ref, o_ref):
    x = x_ref[...]
    rows = x.shape[0]
    gw = 2 * UP_GROUP
    sub = lax.broadcasted_iota(jnp.int32, (SUBLANES, gw), 0)
    short_seqs = pl.program_id(0) >= n_seq_tiles
    for c in range(w_ref.shape[1] // gw):
        cols = slice(c * gw, (c + 1) * gw)
        ocols = slice(c * UP_GROUP, (c + 1) * UP_GROUP)
        half = rows // 2
        u = jnp.concatenate([_dot(x[:half], w_ref[:, cols]), _dot(x[half:], w_ref[:, cols])], axis=0)
        cw0, cw1, cw2, cb = cw_ref[0:1, cols], cw_ref[1:2, cols], cw_ref[2:3, cols], cb_ref[:, cols]

        def gated(prev, cur, nxt):
            v = prev * cw0 + cur * cw1 + nxt * cw2 + cb
            return (_silu(v[:, UP_GROUP:]) * v[:, :UP_GROUP]).astype(BF16)

        up, dn = pltpu.roll(u, 1, 0), pltpu.roll(u, rows - 1, 0)
        o_ref[:, ocols] = gated(up, u, dn)
        for b in range(0, rows + 1, short):
            if b < rows:
                head = slice(b, b + SUBLANES)
                before = 0.0 if b == 0 else jnp.where(
                    short_seqs, 0.0, pltpu.roll(u[b - SUBLANES:b], 1, 0))
                up8 = jnp.where(sub == 0, before, pltpu.roll(u[head], 1, 0))
                o_ref[head, ocols] = gated(up8, u[head], dn[head])
            if b > 0:
                tail = slice(b - SUBLANES, b)
                after = 0.0 if b == rows else jnp.where(
                    short_seqs, 0.0, pltpu.roll(u[b:b + SUBLANES], SUBLANES - 1, 0))
                dn8 = jnp.where(sub == SUBLANES - 1, after, pltpu.roll(u[tail], SUBLANES - 1, 0))
                o_ref[tail, ocols] = gated(up[tail], u[tail], dn8)


def _rope_tables(t, n_ident):
    rows = t // GRID_W
    row = jnp.repeat(jnp.arange(rows, dtype=F32), GRID_W)
    col = jnp.tile(jnp.arange(GRID_W, dtype=F32), rows)
    axis_dim = HEAD_DIM // 2
    inv = ROPE_BASE ** (-jnp.arange(0, axis_dim, 2, dtype=F32) / axis_dim)
    ang = jnp.concatenate([row[:, None] * inv, row[:, None] * inv,
                           col[:, None] * inv, col[:, None] * inv], axis=-1)
    lane = np.arange(HEAD_DIM)
    second_half = jnp.asarray((lane // (axis_dim // 2)) % 2 == 1)
    cos, sin = jnp.cos(ang), jnp.sin(ang)
    sa = jnp.where(second_half, sin, 0.0)
    sb = jnp.where(second_half, 0.0, -sin)
    pad = lambda a, v: jnp.concatenate([a, jnp.full((n_ident, HEAD_DIM), v, F32)], axis=0)
    return pad(cos, 1.0), pad(sa, 0.0), pad(sb, 0.0)


def _dft_constants(t):
    def cs(n):
        k = np.arange(n, dtype=np.int64)
        ph = 2.0 * np.pi * ((k[:, None] * k[None, :]) % n).astype(np.float64) / n
        return np.cos(ph), np.sin(ph)
    ct, st = cs(t)
    cg, sg = cs(FOURIER_GROUP_DIM)
    to_bf16 = lambda a: jnp.asarray(a.astype(np.float32)).astype(BF16)
    return to_bf16(np.concatenate([ct, -st], axis=1)), to_bf16(cg), to_bf16(sg)


def kernel(x, c, ctx, c_ctx, w_ada, b_ada, w_in, sink, w_four, w_out, ln_g, ln_b, w_up, conv_w,
           conv_b, w_down):
    bsz, t, d = x.shape
    n_ctx = ctx.shape[1]
    n_layers = w_ada.shape[0]
    d_f = w_four.shape[1]
    n_groups = d_f // FOURIER_GROUP_DIM
    d_q = d - d_f
    n_q = d_q // HEAD_DIM
    n_kv = n_q // GQA_GROUP
    d_kv = n_kv * HEAD_DIM
    d_in = d_f + d_q + 2 * d_kv
    d_ff = w_down.shape[1]
    alpha = (2 * n_layers) ** 0.25
    scale = HEAD_DIM ** -0.5

    n_lat, n_cx = bsz * t, bsz * n_ctx
    n_tok = n_lat + n_cx
    tm = min(1024, t, n_cx)
    tq = WINDOW
    assert t % tm == 0 and n_cx % tm == 0
    assert t % n_ctx == 0 and n_cx % t == 0 and t >= 3 * tq and t % GRID_W == 0
    assert w_in.shape[2] == d_in and d_ff % 256 == 0
    tiles_per_seq = t // tm
    lat_tiles = n_lat // tm
    cx_blk0 = n_lat // n_ctx

    n_in_blocks = d_in // d_f
    w_in_b = _cast_permute_cols(w_in, d_f, lambda p: (p + 1) % n_in_blocks, "cast_w_in")
    w_out_f = w_out[:, :d_f].astype(BF16)
    w_out_a = w_out[:, d_f:].astype(BF16)
    w_four_b = w_four.astype(BF16)
    n_up_groups = d_ff // UP_GROUP
    w_up_b = _cast_permute_cols(w_up, UP_GROUP, lambda p: p // 2 + (p % 2) * n_up_groups, "cast_w_up")
    w_down_b = w_down.astype(BF16)
    ln_g4 = ln_g.reshape(n_layers, 2, 1, d)
    ln_b4 = ln_b.reshape(n_layers, 2, 1, d)
    conv_w3 = _interleave_value_gate(conv_w, d_ff)
    conv_b3 = _interleave_value_gate(conv_b, d_ff).reshape(n_layers, 1, 2 * d_ff)

    cos_t, sa_t, sb_t = _rope_tables(t, tm)
    dft_lat = _dft_constants(t)
    dft_cx = _dft_constants(n_ctx)
    band_bias = _band_bias(t, tq)

    n_mod_rows = -(-(bsz + 1) // SUBLANES) * SUBLANES
    c_all = jnp.concatenate(
        [c, c_ctx[None, :], jnp.zeros((n_mod_rows - bsz - 1, d), F32)], axis=0)
    mods = _ada_mods(c_all, w_ada, b_ada, tn=min(1024, d)).reshape(n_layers, n_mod_rows, 1, N_MOD * d)

    def mod_row(i, tile, n_lat_tiles):
        return jnp.where(i < n_lat_tiles, i // (t // tile), bsz)

    def mod_spec(l, chunk, tile, n_lat_tiles, grid_rank=1):
        if grid_rank == 1:
            return pl.BlockSpec((None, None, 1, d),
                                lambda i: (l, mod_row(i, tile, n_lat_tiles), 0, chunk))
        return pl.BlockSpec((None, None, 1, d),
                            lambda i, k: (l, mod_row(i, tile, n_lat_tiles), 0, chunk))

    def ln_spec(l, which, grid_rank=1):
        if grid_rank == 1:
            return pl.BlockSpec((None, None, 1, d), lambda i: (l, which, 0, 0))
        return pl.BlockSpec((None, None, 1, d), lambda i, k: (l, which, 0, 0))

    x_lat, x_cx = x.reshape(n_lat, d), ctx.reshape(n_cx, d)

    def two_source_specs(tile, width=d):
        n_lat_tiles = n_lat // tile
        return [pl.BlockSpec((tile, width), lambda i: (jnp.minimum(i, n_lat_tiles - 1), 0)),
                pl.BlockSpec((tile, width), lambda i: (jnp.maximum(i - n_lat_tiles, 0), 0))]

    h = pl.pallas_call(
        functools.partial(_modulate_kernel, lat_tiles),
        grid=(n_tok // tm,),
        in_specs=two_source_specs(tm) + [mod_spec(0, 1, tm, lat_tiles), mod_spec(0, 0, tm, lat_tiles)],
        out_specs=pl.BlockSpec((tm, d), lambda i: (i, 0)),
        out_shape=jax.ShapeDtypeStruct((n_tok, d), BF16),
        compiler_params=_cparams("parallel"),
        name="modulate0",
    )(x_lat, x_cx, mods, mods)
    xs = None

    for l in range(n_layers):
        last = l == n_layers - 1
        rows_out = n_lat if last else n_tok

        qkvf = pl.pallas_call(
            functools.partial(_inproj_kernel, (d_q + d_kv) // HEAD_DIM,
                              next(w for w in (512, 256, 128) if d_in % w == 0)),
            grid=(n_tok // tm,),
            in_specs=[
                pl.BlockSpec((tm, d), lambda i: (i, 0)),
                pl.BlockSpec((None, d, d_in), lambda i: (l, 0, 0), pipeline_mode=pl.Buffered(1)),
            ] + [pl.BlockSpec((tm, HEAD_DIM),
                              lambda i: (jnp.where(i < lat_tiles, i % tiles_per_seq, tiles_per_seq), 0))
                 ] * 3,
            out_specs=pl.BlockSpec((tm, d_in), lambda i: (i, 0)),
            out_shape=jax.ShapeDtypeStruct((n_tok, d_in), BF16),
            compiler_params=_cparams("parallel"),
            name=f"inproj{l}",
        )(h, w_in_b, cos_t, sa_t, sb_t)

        f_col = (d_q + 2 * d_kv) // d_f

        def fourier(seq, n_seq, blk0, consts):
            dft, cg, sg = consts
            return pl.pallas_call(
                functools.partial(_fourier_kernel, seq, n_groups, min(512, seq),
                                  1.0 / float(np.sqrt(seq * FOURIER_GROUP_DIM))),
                grid=(n_seq,),
                in_specs=[
                    pl.BlockSpec((seq, d_f), lambda b: (blk0 + b, f_col)),
                    pl.BlockSpec((FOURIER_GROUP_DIM, FOURIER_GROUP_DIM), lambda b: (0, 0)),
                    pl.BlockSpec((FOURIER_GROUP_DIM, FOURIER_GROUP_DIM), lambda b: (0, 0)),
                    pl.BlockSpec((seq, 2 * seq), lambda b: (0, 0), pipeline_mode=pl.Buffered(1)),
                    pl.BlockSpec((None, d_f, d_f), lambda b: (l, 0, 0)),
                ],
                out_specs=pl.BlockSpec((seq, d_f), lambda b: (b, 0)),
                out_shape=jax.ShapeDtypeStruct((n_seq * seq, d_f), BF16),
                scratch_shapes=[pltpu.VMEM((2 * seq, d_f), BF16)],
                compiler_params=_cparams("parallel"),
                name=f"fourier{l}_{seq}",
            )(qkvf, cg, sg, dft, w_four_b)

        yf = [fourier(t, bsz, 0, dft_lat)]
        if not last:
            yf.append(fourier(n_ctx, bsz, cx_blk0, dft_cx))

        kc_spec = pl.BlockSpec((n_ctx, HEAD_DIM), lambda b, hh: (cx_blk0 + b, n_q + hh))
        vc_spec = pl.BlockSpec((n_ctx, HEAD_DIM), lambda b, hh: (cx_blk0 + b, n_q + n_kv + hh))
        smem_spec = pl.BlockSpec(memory_space=pltpu.SMEM)
        ya = [pl.pallas_call(
            functools.partial(_attn_kernel, t, tq, n_ctx, True, scale),
            grid=(bsz, n_kv),
            in_specs=[
                smem_spec,
                pl.BlockSpec((t, GQA_GROUP * HEAD_DIM), lambda b, hh: (b, hh)),
                pl.BlockSpec((t, HEAD_DIM), lambda b, hh: (b, n_q + hh)),
                pl.BlockSpec((t, HEAD_DIM), lambda b, hh: (b, n_q + n_kv + hh)),
                kc_spec, vc_spec,
                pl.BlockSpec((3, tq, 3 * tq), lambda b, hh: (0, 0, 0)),
            ],
            out_specs=pl.BlockSpec((t, GQA_GROUP * HEAD_DIM), lambda b, hh: (b, hh)),
            out_shape=jax.ShapeDtypeStruct((n_lat, d_q), BF16),
            compiler_params=_cparams("parallel", "parallel"),
            name=f"attn{l}",
        )(sink[l], qkvf, qkvf, qkvf, qkvf, qkvf, band_bias)]
        if not last:
            ya.append(pl.pallas_call(
                functools.partial(_attn_kernel, n_ctx, min(tq, n_ctx), n_ctx, False, scale),
                grid=(bsz, n_kv),
                in_specs=[
                    smem_spec,
                    pl.BlockSpec((n_ctx, GQA_GROUP * HEAD_DIM), lambda b, hh: (cx_blk0 + b, hh)),
                    kc_spec, vc_spec,
                ],
                out_specs=pl.BlockSpec((n_ctx, GQA_GROUP * HEAD_DIM), lambda b, hh: (b, hh)),
                out_shape=jax.ShapeDtypeStruct((n_cx, d_q), BF16),
                compiler_params=_cparams("parallel", "parallel"),
                name=f"attn_ctx{l}",
            )(sink[l], qkvf, qkvf, qkvf))

        tp = min(512, tm)
        lat_tiles_p = n_lat // tp

        def row_specs(arrays, width):
            if len(arrays) == 2:
                return two_source_specs(tp, width)
            return [pl.BlockSpec((tp, width), lambda i: (i, 0))]

        x_args = [x_lat, x_cx] if xs is None else [xs]
        xs, h = pl.pallas_call(
            functools.partial(_outproj_ln_kernel, alpha, len(ya), len(x_args), lat_tiles_p),
            grid=(rows_out // tp,),
            in_specs=row_specs(ya, d_q) + row_specs(yf, d_f) + [
                pl.BlockSpec((None, d_q, d), lambda i: (l, 0, 0), pipeline_mode=pl.Buffered(1)),
                pl.BlockSpec((None, d_f, d), lambda i: (l, 0, 0), pipeline_mode=pl.Buffered(1)),
            ] + row_specs(x_args, d) + [
                mod_spec(l, 2, tp, lat_tiles_p), ln_spec(l, 0), ln_spec(l, 0),
                mod_spec(l, 4, tp, lat_tiles_p), mod_spec(l, 3, tp, lat_tiles_p),
            ],
            out_specs=[pl.BlockSpec((tp, d), lambda i: (i, 0)),
                       pl.BlockSpec((tp, d), lambda i: (i, 0))],
            out_shape=[jax.ShapeDtypeStruct((rows_out, d), F32),
                       jax.ShapeDtypeStruct((rows_out, d), BF16)],
            compiler_params=_cparams("parallel"),
            name=f"outproj_ln{l}",
        )(*ya, *yf, w_out_a, w_out_f, *x_args, mods, ln_g4, ln_b4, mods, mods)

        tn = (2 if d_ff % (2 * UP_GROUP) == 0 else 1) * UP_GROUP
        act = pl.pallas_call(
            functools.partial(_upconv_kernel, bsz, n_ctx),
            grid=(rows_out // t, d_ff // tn),
            in_specs=[
                pl.BlockSpec((t, d), lambda i, j: (i, 0)),
                pl.BlockSpec((None, d, 2 * tn), lambda i, j: (l, 0, j)),
                pl.BlockSpec((None, 3, 2 * tn), lambda i, j: (l, 0, j)),
                pl.BlockSpec((None, 1, 2 * tn), lambda i, j: (l, 0, j)),
            ],
            out_specs=pl.BlockSpec((t, tn), lambda i, j: (i, j)),
            out_shape=jax.ShapeDtypeStruct((rows_out, d_ff), BF16),
            compiler_params=_cparams("parallel", "arbitrary"),
            name=f"upconv{l}",
        )(h, w_up_b, conv_w3, conv_b3)

        tk = d_ff
        td = tp
        lat_tiles_d = n_lat // td
        in_specs = [
            pl.BlockSpec((td, tk), lambda i, k: (i, k)),
            pl.BlockSpec((None, tk, d), lambda i, k: (l, k, 0), pipeline_mode=pl.Buffered(1)),
            pl.BlockSpec((td, d), lambda i, k: (i, 0)),
            mod_spec(l, 5, td, lat_tiles_d, 2), ln_spec(l, 1, 2), ln_spec(l, 1, 2),
        ]
        args = [act, w_down_b, xs, mods, ln_g4, ln_b4]
        out_specs = [pl.BlockSpec((td, d), lambda i, k: (i, 0))]
        out_shape = [jax.ShapeDtypeStruct((rows_out, d), F32)]
        if not last:
            in_specs += [mod_spec(l + 1, 1, td, lat_tiles_d, 2), mod_spec(l + 1, 0, td, lat_tiles_d, 2)]
            args += [mods, mods]
            out_specs.append(pl.BlockSpec((td, d), lambda i, k: (i, 0)))
            out_shape.append(jax.ShapeDtypeStruct((rows_out, d), BF16))
        res = pl.pallas_call(
            functools.partial(_down_ln_kernel, alpha, not last, d_ff // tk),
            grid=(rows_out // td, d_ff // tk),
            in_specs=in_specs,
            out_specs=out_specs,
            out_shape=out_shape,
            scratch_shapes=[pltpu.VMEM((td if d_ff // tk > 1 else SUBLANES, d), F32)],
            compiler_params=_cparams("parallel", "arbitrary", vmem_limit_bytes=DOWN_VMEM_LIMIT_BYTES),
            name=f"down_ln{l}",
        )(*args)
        if last:
            xs = res[0]
        else:
            xs, h = res

    return xs.reshape(bsz, t, d)
```
